```python
import functools
import jax
import jax.numpy as jnp
from jax import lax
import numpy as np

D_MODEL = 1024
BATCH = 4
SEQ = 4096
DEPTH = 2
DEC_BATCH = 32
DEC_SEQ = 4
PAST_LEN = 8192
PAGE_SIZE = 128

CHUNK_A = 128
G_A = 4
DG_A = 128
D_A = G_A * DG_A
H_B = 8
DH_B = 64
D_B = H_B * DH_B
Q_BLOCK = 128
H_C = 4
DK_C = 64
DV_C = 128
DQK_C = H_C * DK_C
D_C = H_C * DV_C
CHUNK_C = 128
N_BRANCH = 3
BRANCH_W = 512
N_SUB = 3
D_FF = 2816
FFN_RES = 0.5
EPS = 1e-6
FORGET_BIAS = 3.0
IN_SPLITS = (D_A, D_A, D_B, D_B, D_B, H_B, DQK_C, DQK_C, D_C, H_C, H_C, D_C, N_BRANCH * D_MODEL)
D_IN = 2 * D_A + 3 * D_B + H_B + 2 * DQK_C + 2 * D_C + 2 * H_C + N_BRANCH * D_MODEL

kernel_name = 'hybrid_fox_mlstm_chunkmlp_step'


def rmsnorm(x, g):
    xf = x.astype(jnp.float32)
    y = xf * lax.rsqrt(jnp.mean(xf * xf, axis=-1, keepdims=True) + EPS)
    return (y * g.astype(jnp.float32)).astype(x.dtype)


def modulated_norm(x, g, shift, scale):
    return rmsnorm(x, g) * (1 + scale[:, None]) + shift[:, None]


def half_ffn(x, mod, g, w_up, w_down):
    h = modulated_norm(x, g, mod[:, 0], mod[:, 1])
    a, b = jnp.split(h @ w_up, 2, axis=-1)
    return x + FFN_RES * mod[:, 2][:, None] * ((jax.nn.silu(a) * b) @ w_down)


def project_in(h, w_in, g_va, g_qb, g_kb, b_fb, b_ic, b_fc):
    B, L, _ = h.shape
    points = np.cumsum(IN_SPLITS)[:-1].tolist()
    ua, va, qb, kb, vb, fb, qc, kc, vc, ic, fc, oc, gt = jnp.split(h @ w_in, points, axis=-1)
    ua = jax.nn.gelu(ua)
    va = rmsnorm(jax.nn.gelu(va), g_va)
    qb = rmsnorm(qb.reshape(B, L, H_B, DH_B), g_qb)
    kb = rmsnorm(kb.reshape(B, L, H_B, DH_B), g_kb)
    vb = vb.reshape(B, L, H_B, DH_B)
    lfb = jax.nn.log_sigmoid((fb + b_fb).astype(jnp.float32))
    qc = qc.reshape(B, L, H_C, DK_C)
    kc = kc.reshape(B, L, H_C, DK_C) * (DK_C ** -0.5)
    vc = vc.reshape(B, L, H_C, DV_C)
    ic = (ic + b_ic).astype(jnp.float32)
    lfc = jax.nn.log_sigmoid((fc + b_fc).astype(jnp.float32))
    oc = jax.nn.sigmoid(oc)
    gates = jax.nn.sigmoid(gt).reshape(B, L, N_BRANCH, D_MODEL)
    return ua, va, qb, kb, vb, lfb, qc, kc, vc, ic, lfc, oc, gates


def spatial_gate(u, v, w_s, b_s):
    B, S, _ = v.shape
    L = min(S, CHUNK_A)
    nc = S // L
    w = jnp.where(jnp.tril(jnp.ones((L, L), bool)), w_s[:, :L, :L], 0.0)
    vc = v.reshape(B, nc, L, G_A, DG_A)
    mix = jnp.einsum('gts,bnsgc->bntgc', w, vc) + b_s[:, :L].T[None, None, :, :, None]
    return u * mix.reshape(B, S, D_A)


def fox_block(q, k, v, fq, fk, qpos, kpos):
    s = jnp.einsum('bqhd,bkhd->bhqk', q.astype(jnp.float32), k) * (DH_B ** -0.5)
    s = s + (jnp.swapaxes(fq, 1, 2)[..., :, None] - jnp.swapaxes(fk, 1, 2)[..., None, :])
    s = jnp.where(kpos[None, None, None, :] <= qpos[None, None, :, None], s, -jnp.inf)
    p = jax.nn.softmax(s, axis=-1)
    return jnp.einsum('bhqk,bkhd->bqhd', p, v)


def fox_prompt(q, k, v, logf):
    B, S, H, _ = q.shape
    nb = S // Q_BLOCK
    F = jnp.cumsum(logf, axis=1)
    kpos = jnp.arange(S)
    kf = k.astype(jnp.float32)
    vf = v.astype(jnp.float32)

    def block(args):
        qblk, fblk, qpos = args
        return fox_block(qblk, kf, vf, fblk, F, qpos, kpos)

    qs = q.reshape(B, nb, Q_BLOCK, H, DH_B).swapaxes(0, 1)
    fs = F.reshape(B, nb, Q_BLOCK, H).swapaxes(0, 1)
    ps = kpos.reshape(nb, Q_BLOCK)
    out = lax.map(block, (qs, fs, ps))
    return out.swapaxes(0, 1).reshape(B, S, D_B).astype(v.dtype)


def fox_sample(q, k_new, v_new, lf_new, k_pages, v_pages, lf_pages):
    Bd, n_pages = k_pages.shape[:2]
    past = n_pages * PAGE_SIZE
    L = q.shape[1]
    k = jnp.concatenate([k_pages.reshape(Bd, past, H_B, DH_B), k_new], axis=1).astype(jnp.float32)
    v = jnp.concatenate([v_pages.reshape(Bd, past, H_B, DH_B), v_new], axis=1).astype(jnp.float32)
    lf = jnp.concatenate([lf_pages.reshape(Bd, past, H_B).astype(jnp.float32), lf_new], axis=1)
    F = jnp.cumsum(lf, axis=1)
    kpos = jnp.arange(past + L)
    out = fox_block(q, k, v, F[:, past:], F, kpos[past:], kpos)
    return out.reshape(Bd, L, D_B).astype(v_new.dtype)


def mlstm_chunk(carry, xs):
    C, n, m = carry
    q, k, v, ig, lf = xs
    L = q.shape[1]
    b = jnp.cumsum(lf, axis=1).swapaxes(1, 2)
    i = ig.swapaxes(1, 2)
    causal = jnp.tril(jnp.ones((L, L), bool))
    dmat = jnp.where(causal, b[..., :, None] - b[..., None, :] + i[..., None, :], -jnp.inf)
    inter = b + m[..., None]
    m_t = jnp.maximum(jnp.max(dmat, axis=-1), inter)
    w_intra = jnp.exp(dmat - m_t[..., None]) * jnp.einsum('blhd,bshd->bhls', q, k)
    w_inter = jnp.exp(inter - m_t)
    num = jnp.einsum('bhls,bshv->bhlv', w_intra, v) + w_inter[..., None] * jnp.einsum('bhvd,blhd->bhlv', C, q)
    den = jnp.sum(w_intra, axis=-1) + w_inter * jnp.einsum('bhd,blhd->bhl', n, q)
    h = num / jnp.maximum(jnp.abs(den), jnp.exp(-m_t))[..., None]
    m_new = m_t[..., -1]
    w_k = jnp.exp(b[..., -1:] - b + i - m_new[..., None])
    decay = jnp.exp(b[..., -1] + m - m_new)
    C_new = decay[..., None, None] * C + jnp.einsum('bhs,bshv,bshd->bhvd', w_k, v, k)
    n_new = decay[..., None] * n + jnp.einsum('bhs,bshd->bhd', w_k, k)
    return (C_new, n_new, m_new), h.swapaxes(1, 2)


def mlstm_prompt(q, k, v, ig, lf):
    B, S = q.shape[:2]
    nc = S // CHUNK_C

    def to_chunks(a):
        return a.astype(jnp.float32).reshape((B, nc, CHUNK_C) + a.shape[2:]).swapaxes(0, 1)

    init = (jnp.zeros((B, H_C, DV_C, DK_C), jnp.float32), jnp.zeros((B, H_C, DK_C), jnp.float32),
            jnp.zeros((B, H_C), jnp.float32))
    state, h = lax.scan(mlstm_chunk, init, (to_chunks(q), to_chunks(k), to_chunks(v), to_chunks(ig), to_chunks(lf)))
    return state, h.swapaxes(0, 1).reshape(B, S, D_C)


def prompt_mix(qb, kb, vb, lfb, qc, kc, vc, ic, lfc):
    yb = fox_prompt(qb, kb, vb, lfb)
    (C, n, m), hc = mlstm_prompt(qc, kc, vc, ic, lfc)
    return yb, hc.astype(vc.dtype), (kb, vb, lfb, C, n, m)


def sample_mix(qb, kb, vb, lfb, qc, kc, vc, ic, lfc, *, layer, cache_k, cache_v, cache_logf,
               page_table, state_C, state_n, state_m):
    yb = fox_sample(qb, kb, vb, lfb, cache_k[layer, page_table], cache_v[layer, page_table],
                    cache_logf[layer, page_table])
    carry = (state_C[layer].astype(jnp.float32), state_n[layer].astype(jnp.float32),
             state_m[layer].astype(jnp.float32))
    f32 = jnp.float32
    (C, n, m), hc = mlstm_chunk(carry, (qc.astype(f32), kc.astype(f32), vc.astype(f32), ic, lfc))
    B, L = qc.shape[:2]
    return yb, hc.reshape(B, L, D_C).astype(vc.dtype), (kb, vb, lfb, C, n, m)


def merge_branches(ys, gates, w_branch, w_out):
    m = gates[:, :, 0] * (ys[0] @ w_branch[0])
    for nb in range(1, N_BRANCH):
        m = m + gates[:, :, nb] * (ys[nb] @ w_branch[nb])
    return m @ w_out


def trunk_layer(x, c, lw, token_mix):
    (w_ada, b_ada, g_norm, w_ff_up, w_ff_down, w_in, g_va, w_s, b_s,
     g_qb, g_kb, b_fb, b_ic, b_fc, w_branch, w_out) = lw
    mod = (jax.nn.silu(c) @ w_ada + b_ada).reshape(c.shape[0], N_SUB, 3, D_MODEL)
    x = half_ffn(x, mod[:, 0], g_norm[0], w_ff_up[0], w_ff_down[0])
    h = modulated_norm(x, g_norm[1], mod[:, 1, 0], mod[:, 1, 1])
    ua, va, qb, kb, vb, lfb, qc, kc, vc, ic, lfc, oc, gates = project_in(h, w_in, g_va, g_qb, g_kb, b_fb, b_ic, b_fc)
    ya = spatial_gate(ua, va, w_s, b_s)
    yb, hc, state = token_mix(qb, kb, vb, lfb, qc, kc, vc, ic, lfc)
    yc = oc * hc
    x = x + mod[:, 1, 2][:, None] * merge_branches((ya, yb, yc), gates, w_branch, w_out)
    x = half_ffn(x, mod[:, 2], g_norm[2], w_ff_up[1], w_ff_down[1])
    return x, va, state


def setup_inputs(seed: int = 0) -> dict:
    key = jax.random.key(seed)
    ks = list(jax.random.split(key, 32))
    n_pages = PAST_LEN // PAGE_SIZE
    n_used = DEC_BATCH * n_pages
    n_pool = (n_used * 5) // 4

    def nrm(i, shape, scale):
        return scale * jax.random.normal(ks[i], shape, jnp.float32)

    x_prompt = nrm(0, (BATCH, SEQ, D_MODEL), 1.0)
    x_sample = nrm(1, (DEC_BATCH, DEC_SEQ, D_MODEL), 1.0)
    cache_k = nrm(2, (DEPTH, n_pool, PAGE_SIZE, H_B, DH_B), 1.0)
    cache_v = nrm(3, (DEPTH, n_pool, PAGE_SIZE, H_B, DH_B), 1.0)
    cache_logf = jax.nn.log_sigmoid(FORGET_BIAS + nrm(4, (DEPTH, n_pool, PAGE_SIZE, H_B), 1.0))
    state_C = nrm(5, (DEPTH, DEC_BATCH, H_C, DV_C, DK_C), 0.3)
    state_n = nrm(6, (DEPTH, DEC_BATCH, H_C, DK_C), 0.3)
    state_m = nrm(7, (DEPTH, DEC_BATCH, H_C), 1.0)
    page_table = jax.random.permutation(ks[8], n_pool)[:n_used].reshape(DEC_BATCH, n_pages).astype(jnp.int32)
    c_prompt = nrm(9, (BATCH, D_MODEL), 1.0)
    c_sample = nrm(10, (DEC_BATCH, D_MODEL), 1.0)
    w_ada = nrm(11, (DEPTH, D_MODEL, N_SUB * 3 * D_MODEL), 0.5 * D_MODEL ** -0.5)
    b_ada = nrm(12, (DEPTH, N_SUB * 3 * D_MODEL), 0.02)
    g_norm = 1.0 + nrm(13, (DEPTH, N_SUB, D_MODEL), 0.1)
    w_ff_up = nrm(14, (DEPTH, 2, D_MODEL, 2 * D_FF), D_MODEL ** -0.5)
    w_ff_down = nrm(15, (DEPTH, 2, D_FF, D_MODEL), D_FF ** -0.5)
    w_in = nrm(16, (DEPTH, D_MODEL, D_IN), D_MODEL ** -0.5)
    g_va = 1.0 + nrm(17, (DEPTH, D_A), 0.1)
    w_s = nrm(18, (DEPTH, G_A, CHUNK_A, CHUNK_A), CHUNK_A ** -0.5)
    b_s = 1.0 + nrm(19, (DEPTH, G_A, CHUNK_A), 0.1)
    g_qb = 1.0 + nrm(20, (DEPTH, DH_B), 0.1)
    g_kb = 1.0 + nrm(21, (DEPTH, DH_B), 0.1)
    b_fb = FORGET_BIAS + nrm(22, (DEPTH, H_B), 0.5)
    b_ic = nrm(23, (DEPTH, H_C), 0.1)
    b_fc = FORGET_BIAS + nrm(24, (DEPTH, H_C), 0.5)
    w_branch = nrm(25, (DEPTH, N_BRANCH, BRANCH_W, D_MODEL), BRANCH_W ** -0.5)
    w_out = nrm(26, (DEPTH, D_MODEL, D_MODEL), D_MODEL ** -0.5)
    return {'x_prompt': x_prompt, 'x_sample': x_sample, 'cache_k': cache_k, 'cache_v': cache_v,
            'cache_logf': cache_logf, 'state_C': state_C, 'state_n': state_n, 'state_m': state_m,
            'page_table': page_table, 'c_prompt': c_prompt, 'c_sample': c_sample,
            'w_ada': w_ada, 'b_ada': b_ada, 'g_norm': g_norm, 'w_ff_up': w_ff_up, 'w_ff_down': w_ff_down,
            'w_in': w_in, 'g_va': g_va, 'w_s': w_s, 'b_s': b_s, 'g_qb': g_qb, 'g_kb': g_kb,
            'b_fb': b_fb, 'b_ic': b_ic, 'b_fc': b_fc, 'w_branch': w_branch, 'w_out': w_out}


def reference(x_prompt, x_sample, cache_k, cache_v, cache_logf, state_C, state_n, state_m, page_table,
              c_prompt, c_sample, w_ada, b_ada, g_norm, w_ff_up, w_ff_down, w_in, g_va, w_s, b_s,
              g_qb, g_kb, b_fb, b_ic, b_fc, w_branch, w_out):
    xp, xs = x_prompt, x_sample
    st_p, st_s, chunk_v = [], [], []
    for l in range(DEPTH):
        lw = (w_ada[l], b_ada[l], g_norm[l], w_ff_up[l], w_ff_down[l], w_in[l], g_va[l], w_s[l], b_s[l],
              g_qb[l], g_kb[l], b_fb[l], b_ic[l], b_fc[l], w_branch[l], w_out[l])
        xp, _, sp = trunk_layer(xp, c_prompt, lw, prompt_mix)
        mix_s = functools.partial(sample_mix, layer=l, cache_k=cache_k, cache_v=cache_v, cache_logf=cache_logf,
                                  page_table=page_table, state_C=state_C, state_n=state_n, state_m=state_m)
        xs, va_s, ss = trunk_layer(xs, c_sample, lw, mix_s)
        st_p.append(sp)
        st_s.append(ss)
        chunk_v.append(va_s)

    def stk(states, i):
        return jnp.stack([s[i] for s in states])

    return (xp, xs,
            stk(st_p, 0), stk(st_p, 1), stk(st_p, 2), stk(st_p, 3), stk(st_p, 4), stk(st_p, 5),
            stk(st_s, 0), stk(st_s, 1), stk(st_s, 2), stk(st_s, 3), stk(st_s, 4), stk(st_s, 5),
            jnp.stack(chunk_v))
```

```python
import functools

import jax
import jax.numpy as jnp
from jax import lax
from jax.experimental import pallas as pl
from jax.experimental.pallas import tpu as pltpu

F32 = jnp.float32
BF16 = jnp.bfloat16
EPS = 1e-6
FFN_RES = 0.5
LANES = 128
MLSTM_CHUNK = 128
VMEM_LIMIT = 56 * 1024 * 1024
NT_DIMS = (((1,), (1,)), ((), ()))


def _dot(a, b):
    return jnp.dot(a, b, preferred_element_type=F32)


def _dot_nt(a, b):
    return lax.dot_general(a, b, NT_DIMS, preferred_element_type=F32)


def _split3(x):
    hi = x.astype(BF16)
    r = x - hi.astype(F32)
    mid = r.astype(BF16)
    lo = (r - mid.astype(F32)).astype(BF16)
    return hi, mid, lo


def _dot_exact_l(a, x):
    hi, mid, lo = _split3(x)
    return _dot(a, hi) + _dot(a, mid) + _dot(a, lo)


def _dot_exact_r(x, a):
    hi, mid, lo = _split3(x)
    return _dot(hi, a) + _dot(mid, a) + _dot(lo, a)


def _modnorm(x, g, shift, scale):
    y = x * lax.rsqrt(jnp.mean(x * x, axis=-1, keepdims=True) + EPS)
    return (y * g) * (1 + scale) + shift


def _params(sem):
    return pltpu.CompilerParams(dimension_semantics=sem, vmem_limit_bytes=VMEM_LIMIT)


def _ada_kernel(c_ref, w_ref, b_ref, o_ref):
    s = jax.nn.silu(c_ref[...]).astype(BF16)
    o_ref[...] = _dot(s, w_ref[...].astype(BF16)) + b_ref[...]


def _ada_mod(c_all, w_ada, b_ada):
    depth, d, n = w_ada.shape
    r = c_all.shape[0]
    tn = n // 8
    return pl.pallas_call(
        _ada_kernel,
        grid=(depth, n // tn),
        in_specs=[pl.BlockSpec((r, d), lambda l, j: (0, 0)),
                  pl.BlockSpec((None, d, tn), lambda l, j: (l, 0, j)),
                  pl.BlockSpec((None, 1, tn), lambda l, j: (l, 0, j))],
        out_specs=pl.BlockSpec((None, r, tn), lambda l, j: (l, 0, j)),
        out_shape=jax.ShapeDtypeStruct((depth, r, n), F32),
        compiler_params=_params(("arbitrary", "arbitrary")),
        name="ada_mod",
    )(c_all, w_ada, b_ada.reshape(depth, 1, n))


def _ffn_kernel(x_ref, sh_ref, sc_ref, gt_ref, g_ref, wa_ref, wb_ref, wd_ref, o_ref, h_scr, acc_scr):
    j = pl.program_id(1)

    @pl.when(j == 0)
    def _():
        h = _modnorm(x_ref[...], g_ref[...], sh_ref[...], sc_ref[...])
        h_scr[...] = h.astype(BF16)
        acc_scr[...] = jnp.zeros_like(acc_scr)

    h = h_scr[...]
    a = _dot(h, wa_ref[...])
    b = _dot(h, wb_ref[...])
    act = (jax.nn.silu(a) * b).astype(BF16)
    acc_scr[...] += _dot(act, wd_ref[...])

    @pl.when(j == pl.num_programs(1) - 1)
    def _():
        o_ref[...] = x_ref[...] + (FFN_RES * gt_ref[...]) * acc_scr[...]


def _ffn(x, mod, g, w_up, w_down, l, k, tm, tiles_per_group):
    m, d = x.shape
    dff = w_down.shape[2]
    tf = dff // 2 if (dff // 2) % LANES == 0 else dff
    nj = dff // tf
    r = mod[0].shape[1]
    modspec = pl.BlockSpec((None, r, d), lambda i, j: (i // tiles_per_group, 0, 0))
    return pl.pallas_call(
        _ffn_kernel,
        grid=(m // tm, nj),
        in_specs=[pl.BlockSpec((tm, d), lambda i, j: (i, 0)), modspec, modspec, modspec,
                  pl.BlockSpec((1, d), lambda i, j: (0, 0)),
                  pl.BlockSpec((None, None, d, tf), lambda i, j: (l, k, 0, j)),
                  pl.BlockSpec((None, None, d, tf), lambda i, j: (l, k, 0, j + nj)),
                  pl.BlockSpec((None, None, tf, d), lambda i, j: (l, k, j, 0))],
        out_specs=pl.BlockSpec((tm, d), lambda i, j: (i, 0)),
        out_shape=jax.ShapeDtypeStruct((m, d), F32),
        scratch_shapes=[pltpu.VMEM((tm, d), BF16), pltpu.VMEM((tm, d), F32)],
        compiler_params=_params(("arbitrary", "arbitrary")),
        name="ffn",
    )(x, mod[0], mod[1], mod[2], g, w_up, w_up, w_down)


def _group_rms_lanes(z, gsz):
    n = z.shape[-1]
    zz = z * z
    parts = []
    lane = lax.broadcasted_iota(jnp.int32, (z.shape[0], LANES), 1)
    for p in range(n // LANES):
        blk = zz[:, p * LANES:(p + 1) * LANES]
        scale = jnp.zeros_like(blk)
        for r in range(LANES // gsz):
            sel = (lane >= r * gsz) & (lane < (r + 1) * gsz)
            ms = jnp.sum(jnp.where(sel, blk, 0.0), axis=-1, keepdims=True) / gsz
            scale = jnp.where(sel, lax.rsqrt(ms + EPS), scale)
        parts.append(scale)
    return jnp.concatenate(parts, axis=-1)


def _log_sigmoid(x):
    return jax.nn.log_sigmoid(x)


def _proj_kernel(*refs, prompt, tm, chunk, groups, dh, n_lf, n_ic, emit_va):
    it = iter(refs)
    x_ref, sh_ref, sc_ref, g_ref = next(it), next(it), next(it), next(it)
    wa_ref, wq_ref, wv_ref, wc_ref, ws_ref = next(it), next(it), next(it), next(it), next(it)
    if prompt:
        wkt_ref, wvt_ref, wkct_ref, wst_ref = next(it), next(it), next(it), next(it)
    else:
        wk_ref, wkc_ref = next(it), next(it)
    gva_ref, gq_ref, gk_ref, bsg_ref = next(it), next(it), next(it), next(it)
    if prompt:
        bsgt_ref = next(it)
    wmix_ref, bmix_ref = next(it), next(it)
    ya_ref, q_ref = next(it), next(it)
    if prompt:
        kt32_ref, kt16_ref, vt32_ref, v16_ref = next(it), next(it), next(it), next(it)
        qc_ref, kct_ref, vc_ref, sg_ref, sgt_ref, f_ref, ft_ref = (next(it), next(it), next(it), next(it),
                                                                   next(it), next(it), next(it))
        carry_r, carry_c = next(it), next(it)
        va_ref = None
    else:
        k32_ref, v32_ref, qc_ref, kc_ref, vc_ref, sg_ref = (next(it), next(it), next(it), next(it),
                                                            next(it), next(it))
        va_ref = next(it) if emit_va else None

    h = _modnorm(x_ref[...], g_ref[...], sh_ref[...], sc_ref[...]).astype(BF16)

    za = _dot(h, wa_ref[...])
    da = za.shape[1] // 2
    dg = da // groups
    ua = jax.nn.gelu(za[:, :da])
    vg = jax.nn.gelu(za[:, da:])
    va = (vg * lax.rsqrt(jnp.mean(vg * vg, axis=-1, keepdims=True) + EPS)) * gva_ref[...]
    if va_ref is not None:
        va_ref[...] = va
    vab = va.astype(BF16)
    rr = lax.broadcasted_iota(jnp.int32, (chunk, chunk), 0)
    cc = lax.broadcasted_iota(jnp.int32, (chunk, chunk), 1)
    for gi in range(groups):
        w = jnp.where(cc <= rr, wmix_ref[gi], 0.0).astype(BF16)
        bcol = bmix_ref[:, gi:gi + 1]
        for c in range(tm // chunk):
            rs = slice(c * chunk, (c + 1) * chunk)
            cs = slice(gi * dg, (gi + 1) * dg)
            mix = _dot(w, vab[rs, cs]) + bcol
            ya_ref[rs, cs] = (ua[rs, cs] * mix).astype(BF16)

    zq = _dot(h, wq_ref[...])
    qn = (zq * _group_rms_lanes(zq, dh)) * gq_ref[...]
    q_ref[...] = (qn * (dh ** -0.5)).astype(BF16)

    zv = _dot(h, wv_ref[...])
    if prompt:
        kt = _dot_nt(wkt_ref[...], h)
        n_heads = kt.shape[0] // dh
        for hh in range(n_heads):
            blk = kt[hh * dh:(hh + 1) * dh, :]
            ms = jnp.mean(blk * blk, axis=0, keepdims=True)
            kn = (blk * lax.rsqrt(ms + EPS)) * gk_ref[...]
            kt32_ref[hh * dh:(hh + 1) * dh, :] = kn
            kt16_ref[hh * dh:(hh + 1) * dh, :] = kn.astype(BF16)
        vt32_ref[...] = _dot_nt(wvt_ref[...], h)
        v16_ref[...] = zv.astype(BF16)
    else:
        zk = _dot(h, wk_ref[...])
        k32_ref[...] = (zk * _group_rms_lanes(zk, dh)) * gk_ref[...]
        v32_ref[...] = zv

    zc = _dot(h, wc_ref[...])
    dqk = qc_ref.shape[-1]
    dkc = dqk // (n_ic)
    qc_ref[...] = zc[:, :dqk]
    vc_ref[...] = zc[:, dqk:]
    if prompt:
        kct_ref[...] = _dot_nt(wkct_ref[...], h) * (dkc ** -0.5)
    else:
        kc_ref[...] = _dot(h, wkc_ref[...]) * (dkc ** -0.5)

    zs = _dot(h, ws_ref[...]) + bsg_ref[...]
    col = lax.broadcasted_iota(jnp.int32, zs.shape, 1)
    is_ic = (col >= n_lf) & (col < n_lf + n_ic)
    lsg = jnp.where(is_ic, zs, _log_sigmoid(zs))
    sg_ref[...] = lsg
    if prompt:
        s_idx = pl.program_id(1)

        @pl.when(s_idx == 0)
        def _():
            carry_r[...] = jnp.zeros_like(carry_r)
            carry_c[...] = jnp.zeros_like(carry_c)

        r2 = lax.broadcasted_iota(jnp.int32, (tm, tm), 0)
        c2 = lax.broadcasted_iota(jnp.int32, (tm, tm), 1)
        tril = jnp.where(c2 <= r2, 1.0, 0.0).astype(BF16)
        fcum = _dot_exact_l(tril, lsg) + carry_r[0:1, :]
        f_ref[...] = fcum
        carry_r[0:1, :] = fcum[tm - 1:tm, :]

        zst = _dot_nt(wst_ref[...], h) + bsgt_ref[...]
        row = lax.broadcasted_iota(jnp.int32, zst.shape, 0)
        is_ic_t = (row >= n_lf) & (row < n_lf + n_ic)
        lsgt = jnp.where(is_ic_t, zst, _log_sigmoid(zst))
        sgt_ref[...] = lsgt
        triu = jnp.where(r2 <= c2, 1.0, 0.0).astype(BF16)
        ftc = _dot_exact_r(lsgt, triu) + carry_c[:, 0:1]
        ft_ref[...] = ftc
        carry_c[:, 0:1] = ftc[:, tm - 1:tm]


def _proj(x, mod, g, pw, *, prompt, batch, tm, chunk, groups, dh, n_lf, n_ic, emit_va=False):
    m, d = x.shape
    s_len = m // batch
    nt = s_len // tm
    r = mod[0].shape[1]
    da = pw["wa"].shape[1] // 2
    db = pw["wq"].shape[1]
    dqk = pw["wc"].shape[1] - pw["dvc"]
    dvc = pw["dvc"]
    nsg = pw["ws"].shape[1]
    nsgt = 2 * ((n_lf + 2 * n_ic + 15) // 16) * 8

    def row(i, j):
        return (i * nt + j, 0)

    modspec = pl.BlockSpec((None, r, d), (lambda i, j: (i, 0, 0)) if r == 1 else (lambda i, j: (i * nt + j, 0, 0)))
    full = lambda a: pl.BlockSpec(a.shape, lambda i, j: (0,) * a.ndim)
    ins = [x, mod[0], mod[1], g, pw["wa"], pw["wq"], pw["wv"], pw["wc"], pw["ws"]]
    in_specs = [pl.BlockSpec((tm, d), row), modspec, modspec, full(g),
                full(pw["wa"]), full(pw["wq"]), full(pw["wv"]), full(pw["wc"]), full(pw["ws"])]
    if prompt:
        extra = [pw["wkt"], pw["wvt"], pw["wkct"], pw["wst"]]
    else:
        extra = [pw["wk"], pw["wkc"]]
    ins += extra
    in_specs += [full(a) for a in extra]
    small = [pw["gva"], pw["gq"], pw["gk"], pw["bsg"]] + ([pw["bsgt"]] if prompt else []) + [pw["wmix"], pw["bmix"]]
    ins += small
    in_specs += [full(a) for a in small]

    rowspec = lambda n: pl.BlockSpec((tm, n), row)
    out_shape = [jax.ShapeDtypeStruct((m, da), BF16), jax.ShapeDtypeStruct((m, db), BF16)]
    out_specs = [rowspec(da), rowspec(db)]
    scratch = []
    if prompt:
        colspec = lambda n: pl.BlockSpec((None, n, tm), lambda i, j: (i, 0, j))
        slabspec = lambda n: pl.BlockSpec((None, None, n, tm), lambda i, j: (i, j, 0, 0))
        out_shape += [jax.ShapeDtypeStruct((batch, db, s_len), F32),
                      jax.ShapeDtypeStruct((batch, nt, db, tm), BF16),
                      jax.ShapeDtypeStruct((batch, db, s_len), F32),
                      jax.ShapeDtypeStruct((m, db), BF16),
                      jax.ShapeDtypeStruct((m, dqk), F32),
                      jax.ShapeDtypeStruct((batch, dqk, s_len), F32),
                      jax.ShapeDtypeStruct((m, dvc), F32),
                      jax.ShapeDtypeStruct((m, nsg), F32),
                      jax.ShapeDtypeStruct((batch, nsgt, s_len), F32),
                      jax.ShapeDtypeStruct((m, nsg), F32),
                      jax.ShapeDtypeStruct((batch, nt, nsgt, tm), F32)]
        out_specs += [colspec(db), slabspec(db), colspec(db), rowspec(db), rowspec(dqk), colspec(dqk),
                      rowspec(dvc), rowspec(nsg), colspec(nsgt), rowspec(nsg), slabspec(nsgt)]
        scratch = [pltpu.VMEM((8, nsg), F32), pltpu.VMEM((nsgt, LANES), F32)]
    else:
        out_shape += [jax.ShapeDtypeStruct((m, db), F32), jax.ShapeDtypeStruct((m, db), F32),
                      jax.ShapeDtypeStruct((m, dqk), F32), jax.ShapeDtypeStruct((m, dqk), F32),
                      jax.ShapeDtypeStruct((m, dvc), F32), jax.ShapeDtypeStruct((m, nsg), F32)]
        out_specs += [rowspec(db), rowspec(db), rowspec(dqk), rowspec(dqk), rowspec(dvc), rowspec(nsg)]
        if emit_va:
            out_shape.append(jax.ShapeDtypeStruct((m, da), F32))
            out_specs.append(rowspec(da))
    kern = functools.partial(_proj_kernel, prompt=prompt, tm=tm, chunk=chunk, groups=groups, dh=dh,
                             n_lf=n_lf, n_ic=n_ic, emit_va=emit_va)
    return pl.pallas_call(
        kern, grid=(batch, nt), in_specs=in_specs, out_specs=out_specs, out_shape=out_shape,
        scratch_shapes=scratch, compiler_params=_params(("arbitrary", "arbitrary")),
        name="proj_prompt" if prompt else "proj_sample",
    )(*ins)


def _fox_prompt_kernel(q_ref, kt_ref, v_ref, f_ref, ft_ref, o_ref, m_scr, l_scr, acc_scr, *, tq, tk, dh):
    qi = pl.program_id(1)
    n_pairs = q_ref.shape[1] // LANES
    ratio = tq // tk
    lane = lax.broadcasted_iota(jnp.int32, (tq, LANES), 1)
    lo = lane < dh
    rowg = lax.broadcasted_iota(jnp.int32, (tq, tk), 0)
    colg = lax.broadcasted_iota(jnp.int32, (tq, tk), 1)

    for p in range(n_pairs):
        qp = q_ref[:, p * LANES:(p + 1) * LANES]
        qh = (jnp.where(lo, qp, jnp.zeros_like(qp)), jnp.where(lo, jnp.zeros_like(qp), qp))
        fq = (f_ref[:, 2 * p:2 * p + 1], f_ref[:, 2 * p + 1:2 * p + 2])
        m_scr[...] = jnp.full_like(m_scr, -jnp.inf)
        l_scr[...] = jnp.zeros_like(l_scr)
        acc_scr[...] = jnp.zeros_like(acc_scr)

        def step(kj, diag_off):
            kt = kt_ref[kj, p * LANES:(p + 1) * LANES, :]
            vv = v_ref[pl.ds(pl.multiple_of(kj * tk, tk), tk), p * LANES:(p + 1) * LANES]
            for hh in range(2):
                fk = ft_ref[kj, 2 * p + hh:2 * p + hh + 1, :]
                s = _dot(qh[hh], kt) + (fq[hh] - fk)
                if diag_off is not None:
                    s = jnp.where(colg + diag_off * tk <= rowg, s, -jnp.inf)
                m_prev = m_scr[hh]
                m_new = jnp.maximum(m_prev, jnp.max(s, axis=-1, keepdims=True))
                alpha = jnp.exp(m_prev - m_new)
                pm = jnp.exp(s - m_new)
                l_scr[hh] = alpha * l_scr[hh] + jnp.sum(pm, axis=-1, keepdims=True)
                acc_scr[hh] = alpha * acc_scr[hh] + _dot(pm.astype(BF16), vv)
                m_scr[hh] = m_new

        for d_off in range(ratio):
            step(qi * ratio + d_off, d_off)

        def body(kj, carry):
            step(kj, None)
            return carry

        lax.fori_loop(0, qi * ratio, body, 0)
        out = jnp.where(lo, acc_scr[0] / l_scr[0], acc_scr[1] / l_scr[1])
        o_ref[:, p * LANES:(p + 1) * LANES] = out.astype(BF16)


def _fox_prompt(q, kt, v, f, ft, *, batch, tq, dh):
    m, db = q.shape
    s_len = m // batch
    nk, tk = kt.shape[1], kt.shape[3]
    nq = s_len // tq
    kern = functools.partial(_fox_prompt_kernel, tq=tq, tk=tk, dh=dh)
    return pl.pallas_call(
        kern, grid=(batch, nq),
        in_specs=[pl.BlockSpec((tq, db), lambda b, i: (b * nq + i, 0)),
                  pl.BlockSpec((None, nk, db, tk), lambda b, i: (b, 0, 0, 0)),
                  pl.BlockSpec((s_len, db), lambda b, i: (b, 0)),
                  pl.BlockSpec((tq, f.shape[1]), lambda b, i: (b * nq + i, 0)),
                  pl.BlockSpec((None, nk, ft.shape[2], tk), lambda b, i: (b, 0, 0, 0))],
        out_specs=pl.BlockSpec((tq, db), lambda b, i: (b * nq + i, 0)),
        out_shape=jax.ShapeDtypeStruct((m, db), BF16),
        scratch_shapes=[pltpu.VMEM((2, tq, 1), F32), pltpu.VMEM((2, tq, 1), F32),
                        pltpu.VMEM((2, tq, LANES), F32)],
        compiler_params=_params(("arbitrary", "arbitrary")),
        name="fox_prompt",
    )(q, kt, v, f, ft)


def _fox_sample_kernel(pt_ref, q_ref, kn_ref, vn_ref, lfn_ref, *refs, n_tok, n_heads, dh, pages):
    del pt_ref
    k_refs = refs[:pages]
    v_refs = refs[pages:2 * pages]
    lf_refs = refs[2 * pages:3 * pages]
    o_ref, qs_scr, m_scr, l_scr, acc_scr, r_scr, cq_scr = refs[3 * pages:]
    j = pl.program_id(1)
    nrow = n_tok * n_heads
    dbw = q_ref.shape[-1]
    psz = kn_ref.shape[0]
    hrow = lax.broadcasted_iota(jnp.int32, (n_heads, dbw), 0)
    hcol = lax.broadcasted_iota(jnp.int32, (n_heads, dbw), 1)
    headmask = (hcol >= hrow * dh) & (hcol < (hrow + 1) * dh)
    r2 = lax.broadcasted_iota(jnp.int32, (psz, psz), 0)
    c2 = lax.broadcasted_iota(jnp.int32, (psz, psz), 1)

    def update(s, vmat, nt_form):
        m_prev = m_scr[...]
        m_new = jnp.maximum(m_prev, jnp.max(s, axis=-1, keepdims=True))
        alpha = jnp.exp(m_prev - m_new)
        pm = jnp.exp(s - m_new)
        l_scr[...] = alpha * l_scr[...] + jnp.sum(pm, axis=-1, keepdims=True)
        pv = _dot_nt(pm.astype(BF16), vmat) if nt_form else _dot(pm.astype(BF16), vmat)
        acc_scr[...] = alpha * acc_scr[...] + pv
        m_scr[...] = m_new

    @pl.when(j == 0)
    def _():
        q = q_ref[...]
        qs = jnp.concatenate(
            [jnp.where(headmask, jnp.broadcast_to(q[t:t + 1, :], (n_heads, dbw)), 0.0)
             for t in range(n_tok)], axis=0).astype(BF16)
        qs_scr[...] = qs
        triu = jnp.where(r2 <= c2, 1.0, 0.0).astype(BF16)
        cn = _dot_exact_r(lfn_ref[...], triu)
        cq = jnp.concatenate([cn[:, t:t + 1] for t in range(n_tok)], axis=0)
        cq_scr[...] = cq
        bias = cq - jnp.concatenate([cn] * n_tok, axis=0)
        s = _dot_nt(qs, kn_ref[...]) + bias
        trow = lax.broadcasted_iota(jnp.int32, (nrow, psz), 0) // n_heads
        scol = lax.broadcasted_iota(jnp.int32, (nrow, psz), 1)
        s = jnp.where(scol <= trow, s, -jnp.inf)
        m_scr[...] = jnp.full_like(m_scr, -jnp.inf)
        l_scr[...] = jnp.zeros_like(l_scr)
        acc_scr[...] = jnp.zeros_like(acc_scr)
        r_scr[...] = jnp.zeros_like(r_scr)
        update(s, vn_ref[...], False)

    qs = qs_scr[...]
    cq = cq_scr[...]
    later = jnp.where(r2 > c2, 1.0, 0.0).astype(BF16)
    for i in range(pages):
        lf = lf_refs[i][...]
        suf = _dot_exact_r(lf, later) + r_scr[:, 0:1]
        s = _dot(qs, k_refs[i][...].astype(BF16)) + (cq + jnp.concatenate([suf] * n_tok, axis=0))
        update(s, v_refs[i][...].astype(BF16), True)
        r_scr[:, 0:1] = r_scr[:, 0:1] + jnp.sum(lf, axis=-1, keepdims=True)

    @pl.when(j == pl.num_programs(1) - 1)
    def _():
        out = acc_scr[...] / l_scr[...]
        for t in range(n_tok):
            blk = jnp.where(headmask, out[t * n_heads:(t + 1) * n_heads, :], 0.0)
            o_ref[t:t + 1, :] = jnp.sum(blk, axis=0, keepdims=True).astype(o_ref.dtype)


def _fox_sample(page_table, q, k_new, v_new, lft_new, cache_kt, cache_vt, cache_lft, layer, *, n_heads, dh, pages):
    bd, n_tok, dbw = q.shape
    n_pages = page_table.shape[1]
    psz = cache_kt.shape[-1]
    steps = n_pages // pages
    nrow = n_tok * n_heads

    def page_map(i):
        return lambda b, j, pt: (layer, pt[b, n_pages - 1 - (j * pages + i)], 0, 0)

    per_b = lambda shape: pl.BlockSpec((None,) + shape, lambda b, j, pt: (b, 0, 0))
    in_specs = [per_b((n_tok, dbw)), per_b((psz, dbw)), per_b((psz, dbw)), per_b((n_heads, psz))]
    in_specs += [pl.BlockSpec((None, None, dbw, psz), page_map(i)) for i in range(pages)]
    in_specs += [pl.BlockSpec((None, None, dbw, psz), page_map(i)) for i in range(pages)]
    in_specs += [pl.BlockSpec((None, None, n_heads, psz), page_map(i)) for i in range(pages)]
    kern = functools.partial(_fox_sample_kernel, n_tok=n_tok, n_heads=n_heads, dh=dh, pages=pages)
    grid_spec = pltpu.PrefetchScalarGridSpec(
        num_scalar_prefetch=1, grid=(bd, steps), in_specs=in_specs,
        out_specs=pl.BlockSpec((None, n_tok, dbw), lambda b, j, pt: (b, 0, 0)),
        scratch_shapes=[pltpu.VMEM((nrow, dbw), BF16), pltpu.VMEM((nrow, 1), F32), pltpu.VMEM((nrow, 1), F32),
                        pltpu.VMEM((nrow, dbw), F32), pltpu.VMEM((n_heads, LANES), F32),
                        pltpu.VMEM((nrow, 1), F32)])
    return pl.pallas_call(
        kern, grid_spec=grid_spec, out_shape=jax.ShapeDtypeStruct((bd, n_tok, dbw), F32),
        compiler_params=_params(("arbitrary", "arbitrary")), name="fox_sample",
    )(page_table, q, k_new, v_new, lft_new, *([cache_kt] * pages), *([cache_vt] * pages), *([cache_lft] * pages))


def _mlstm_kernel(q_ref, kt_ref, v_ref, sg_ref, c0_ref, n0_ref, m0_ref, h_ref, st_ref, mo_ref, st_scr, m_scr,
                  *, n_heads, dk, dv, valid, ic_col, lf_col):
    c = pl.program_id(1)
    L = q_ref.shape[0]
    dqk = n_heads * dk

    @pl.when(c == 0)
    def _():
        st_scr[:, :dv] = c0_ref[...]
        st_scr[:, dv:] = jnp.broadcast_to(n0_ref[...], (dqk, st_scr.shape[1] - dv))
        for hh in range(n_heads):
            m_scr[hh] = jnp.broadcast_to(m0_ref[hh:hh + 1, 0:1], (L, L))

    q = q_ref[...].astype(BF16)
    kt = kt_ref[...].astype(BF16)
    sg = sg_ref[...]
    row = lax.broadcasted_iota(jnp.int32, (L, L), 0)
    col = lax.broadcasted_iota(jnp.int32, (L, L), 1)
    tril = jnp.where(col <= row, 1.0, 0.0).astype(BF16)
    ones = jnp.ones((L, L), BF16)
    eye = col == row
    mask = (col <= row) & (col < valid)
    qlane = lax.broadcasted_iota(jnp.int32, (L, dqk), 1)
    st_bf = st_scr[...].astype(BF16)

    for hh in range(n_heads):
        qm = jnp.where((qlane >= hh * dk) & (qlane < (hh + 1) * dk), q, jnp.zeros_like(q))
        kth = kt[hh * dk:(hh + 1) * dk, :]
        vh = v_ref[:, hh * dv:(hh + 1) * dv]
        lf_c = jnp.broadcast_to(sg[:, lf_col + hh:lf_col + hh + 1], (L, L))
        ig_c = jnp.broadcast_to(sg[:, ic_col + hh:ic_col + hh + 1], (L, L))
        b_c = _dot_exact_l(tril, lf_c)
        a_c = ig_c - b_c
        a_r = _dot_exact_l(ones, jnp.where(eye, a_c, 0.0))
        m_old = m_scr[hh]
        dmat = jnp.where(mask, b_c + a_r, -jnp.inf)
        inter = b_c + m_old
        m_t = jnp.maximum(jnp.max(dmat, axis=-1, keepdims=True), inter)
        w_intra = jnp.exp(dmat - m_t) * _dot(qm, kt)
        w_inter = jnp.exp(inter - m_t)
        qcn = _dot(qm, st_bf)
        num = _dot(w_intra.astype(BF16), vh.astype(BF16)) + w_inter[:, 0:1] * qcn[:, :dv]
        den = jnp.sum(w_intra, axis=-1, keepdims=True) + w_inter[:, 0:1] * qcn[:, dv:dv + 1]
        hout = num / jnp.maximum(jnp.abs(den), jnp.exp(-m_t[:, 0:1]))
        h_ref[:, hh * dv:(hh + 1) * dv] = hout

        m_new = jnp.broadcast_to(m_t[valid - 1:valid, :], (L, L))
        b_last = jnp.broadcast_to(b_c[valid - 1:valid, :], (L, L))
        w_k = jnp.where(row < valid, jnp.exp(b_last + a_c - m_new), 0.0)
        decay = jnp.exp(b_last + m_old - m_new)[0:1, 0:1]
        rhs = jnp.concatenate([w_k[:, 0:1] * vh, jnp.broadcast_to(w_k[:, 0:1], (L, st_scr.shape[1] - dv))],
                              axis=1).astype(BF16)
        upd = _dot(kth, rhs)
        st_scr[hh * dk:(hh + 1) * dk, :] = decay * st_scr[hh * dk:(hh + 1) * dk, :] + upd
        m_scr[hh] = m_new

    @pl.when(c == pl.num_programs(1) - 1)
    def _():
        st_ref[...] = st_scr[...]
        for hh in range(n_heads):
            mo_ref[hh:hh + 1, :] = m_scr[hh][0:1, :]


def _mlstm(q, kt, v, sg, c0t, n0, m0, *, n_heads, dk, dv, valid, ic_col, lf_col):
    batch, s_len, dqk = q.shape
    L = MLSTM_CHUNK
    nc = s_len // L
    wst = 2 * dv
    kern = functools.partial(_mlstm_kernel, n_heads=n_heads, dk=dk, dv=dv, valid=valid, ic_col=ic_col, lf_col=lf_col)
    return pl.pallas_call(
        kern, grid=(batch, nc),
        in_specs=[pl.BlockSpec((None, L, dqk), lambda b, c: (b, c, 0)),
                  pl.BlockSpec((None, dqk, L), lambda b, c: (b, 0, c)),
                  pl.BlockSpec((None, L, n_heads * dv), lambda b, c: (b, c, 0)),
                  pl.BlockSpec((None, L, sg.shape[2]), lambda b, c: (b, c, 0)),
                  pl.BlockSpec((None, dqk, dv), lambda b, c: (b, 0, 0)),
                  pl.BlockSpec((None, dqk, 1), lambda b, c: (b, 0, 0)),
                  pl.BlockSpec((None, n_heads, 1), lambda b, c: (b, 0, 0))],
        out_specs=[pl.BlockSpec((None, L, n_heads * dv), lambda b, c: (b, c, 0)),
                   pl.BlockSpec((None, dqk, wst), lambda b, c: (b, 0, 0)),
                   pl.BlockSpec((None, n_heads, L), lambda b, c: (b, 0, 0))],
        out_shape=[jax.ShapeDtypeStruct((batch, s_len, n_heads * dv), F32),
                   jax.ShapeDtypeStruct((batch, dqk, wst), F32),
                   jax.ShapeDtypeStruct((batch, n_heads, L), F32)],
        scratch_shapes=[pltpu.VMEM((dqk, wst), F32), pltpu.VMEM((n_heads, L, L), F32)],
        compiler_params=_params(("arbitrary", "arbitrary")),
        name="mlstm",
    )(q, kt, v, sg, c0t, n0, m0)


def _merge_kernel(x_ref, sh_ref, sc_ref, gt_ref, g_ref, ya_ref, yb_ref, hc_ref,
                  woc_ref, wgt_ref, wbr_ref, wout_ref, o_ref):
    x = x_ref[...]
    h = _modnorm(x, g_ref[...], sh_ref[...], sc_ref[...]).astype(BF16)
    oc = jax.nn.sigmoid(_dot(h, woc_ref[...]))
    ys = (ya_ref[...], yb_ref[...], (oc * hc_ref[...]).astype(BF16))
    d = x.shape[1]
    acc = None
    for nb in range(len(ys)):
        gate = jax.nn.sigmoid(_dot(h, wgt_ref[:, nb * d:(nb + 1) * d]))
        term = gate * _dot(ys[nb], wbr_ref[nb])
        acc = term if acc is None else acc + term
    o_ref[...] = x + gt_ref[...] * _dot(acc.astype(BF16), wout_ref[...])


def _merge(x, mod, g, ya, yb, hc, mw, *, tm, tiles_per_group):
    m, d = x.shape
    r = mod[0].shape[1]
    modspec = pl.BlockSpec((None, r, d), lambda i: (i // tiles_per_group, 0, 0))
    full = lambda a: pl.BlockSpec(a.shape, lambda i: (0,) * a.ndim)
    rowspec = lambda n: pl.BlockSpec((tm, n), lambda i: (i, 0))
    ws = [mw["woc"], mw["wgt"], mw["wbr"], mw["wout"]]
    return pl.pallas_call(
        _merge_kernel, grid=(m // tm,),
        in_specs=[rowspec(d), modspec, modspec, modspec, full(g), rowspec(ya.shape[1]), rowspec(yb.shape[1]),
                  rowspec(hc.shape[1])] + [full(w) for w in ws],
        out_specs=rowspec(d),
        out_shape=jax.ShapeDtypeStruct((m, d), F32),
        compiler_params=_params(("arbitrary",)),
        name="merge",
    )(x, mod[0], mod[1], mod[2], g, ya, yb, hc, *ws)


def _pick_tile(n, pref):
    t = min(n, pref)
    while n % t:
        t //= 2
    return t


def kernel(x_prompt, x_sample, cache_k, cache_v, cache_logf, state_C, state_n, state_m, page_table,
           c_prompt, c_sample, w_ada, b_ada, g_norm, w_ff_up, w_ff_down, w_in, g_va, w_s, b_s,
           g_qb, g_kb, b_fb, b_ic, b_fc, w_branch, w_out):
    B, S, D = x_prompt.shape
    Bd, T, _ = x_sample.shape
    depth = w_ada.shape[0]
    n_pool, psz, HB, dh = cache_k.shape[1:]
    HC, DV, DK = state_C.shape[2:]
    GA, chunk_a = w_s.shape[1], w_s.shape[2]
    DA = g_va.shape[1]
    DB = HB * dh
    DQK = HC * DK
    DC = HC * DV
    NBR, BW = w_branch.shape[1], w_branch.shape[2]
    n_sub = g_norm.shape[1]
    Ms = Bd * T

    splits = (DA, DA, DB, DB, DB, HB, DQK, DQK, DC, HC, HC, DC, NBR * D)
    offs = [0]
    for sz in splits:
        offs.append(offs[-1] + sz)
    (o_ua, o_va, o_qb, o_kb, o_vb, o_fb, o_qc, o_kc, o_vc, o_ic, o_fc, o_oc, o_gt, o_end) = offs

    w_in_t = jnp.swapaxes(w_in, 1, 2)
    nsg = LANES
    nsgt = 2 * ((HB + 2 * HC + 15) // 16) * 8
    n_small = HB + 2 * HC

    def cols(l, a, b):
        return jnp.swapaxes(w_in_t[l, a:b], 0, 1).astype(BF16)

    def rows(l, a, b):
        return w_in_t[l, a:b].astype(BF16)

    w_up16 = w_ff_up.astype(BF16)
    w_down16 = w_ff_down.astype(BF16)
    w_br16 = w_branch.astype(BF16)
    w_out16 = w_out.astype(BF16)

    layer_w = []
    for l in range(depth):
        small_rows = jnp.concatenate([w_in_t[l, o_fb:o_qc], w_in_t[l, o_ic:o_oc]], axis=0)
        wst = jnp.zeros((nsgt, D), F32).at[:n_small].set(small_rows).astype(BF16)
        ws = jnp.zeros((D, nsg), F32).at[:, :n_small].set(small_rows.T).astype(BF16)
        bias_small = jnp.concatenate([b_fb[l], b_ic[l], b_fc[l]])
        bsg = jnp.zeros((1, nsg), F32).at[0, :n_small].set(bias_small)
        bsgt = jnp.zeros((nsgt, 1), F32).at[:n_small, 0].set(bias_small)
        common = dict(
            wa=cols(l, o_ua, o_qb), wq=cols(l, o_qb, o_kb), wv=cols(l, o_vb, o_fb),
            wc=jnp.concatenate([cols(l, o_qc, o_kc), cols(l, o_vc, o_ic)], axis=1), dvc=DC,
            ws=ws, gva=g_va[l].reshape(1, DA), gq=jnp.tile(g_qb[l], HB).reshape(1, DB), bsg=bsg)
        wblk = jnp.einsum("ab,gts->gatbs", jnp.eye(Bd, dtype=F32), w_s[l][:, :T, :T]).reshape(GA, Ms, Ms)
        pw_p = dict(common, wkt=rows(l, o_kb, o_vb), wvt=rows(l, o_vb, o_fb), wkct=rows(l, o_kc, o_vc),
                    wst=wst, gk=g_kb[l].reshape(dh, 1), bsgt=bsgt, wmix=w_s[l], bmix=b_s[l].T)
        pw_s = dict(common, wk=cols(l, o_kb, o_vb), wkc=cols(l, o_kc, o_vc),
                    gk=jnp.tile(g_kb[l], HB).reshape(1, DB), wmix=wblk, bmix=jnp.tile(b_s[l][:, :T], (1, Bd)).T)
        mw = dict(woc=cols(l, o_oc, o_gt), wgt=cols(l, o_gt, o_end), wbr=w_br16[l], wout=w_out16[l])
        layer_w.append((pw_p, pw_s, mw))

    r_all = B + Bd
    r_pad = -(-r_all // 8) * 8
    c_all = jnp.zeros((r_pad, D), F32).at[:B].set(c_prompt).at[B:r_all].set(c_sample)
    mods = _ada_mod(c_all, w_ada, b_ada).reshape(depth, r_pad, n_sub, 3, D)

    cache_kt = jnp.transpose(cache_k, (0, 1, 3, 4, 2)).reshape(depth, n_pool, DB, psz)
    cache_vt = jnp.transpose(cache_v, (0, 1, 3, 4, 2)).reshape(depth, n_pool, DB, psz)
    cache_lft = jnp.transpose(cache_logf, (0, 1, 3, 2))

    tm_p = _pick_tile(S, 512)
    tm_proj = _pick_tile(S, 256)
    tq = _pick_tile(S, 512)
    pages = _pick_tile(page_table.shape[1], 8)

    xp = x_prompt.reshape(B * S, D)
    xs = x_sample.reshape(Ms, D)
    outs_p, outs_s, chunk_v = [], [], []
    for l in range(depth):
        pw_p, pw_s, mw = layer_w[l]

        def mod_p(sub):
            return tuple(mods[l, :B, sub, i].reshape(B, 1, D) for i in range(3))

        def mod_s(sub):
            return tuple(jnp.repeat(mods[l, B:r_all, sub, i], T, axis=0).reshape(1, Ms, D) for i in range(3))

        gn = lambda sub: g_norm[l, sub].reshape(1, D)

        xp = _ffn(xp, mod_p(0), gn(0), w_up16, w_down16, l, 0, tm_p, S // tm_p)
        (ya, q16, kt32, kt16, vt32, v16, qc, kct, vc, sg, sgt, fcum, ftc) = _proj(
            xp, mod_p(1), gn(1), pw_p, prompt=True, batch=B, tm=tm_proj, chunk=chunk_a, groups=GA, dh=dh,
            n_lf=HB, n_ic=HC)
        yb = _fox_prompt(q16, kt16, v16, fcum, ftc, batch=B, tq=tq, dh=dh)
        hc, st, mo = _mlstm(qc.reshape(B, S, DQK), kct, vc.reshape(B, S, DC), sg.reshape(B, S, nsg),
                            jnp.zeros((B, DQK, DV), F32), jnp.zeros((B, DQK, 1), F32), jnp.zeros((B, HC, 1), F32),
                            n_heads=HC, dk=DK, dv=DV, valid=MLSTM_CHUNK, ic_col=HB, lf_col=HB + HC)
        xp = _merge(xp, mod_p(1), gn(1), ya, yb, hc.reshape(B * S, DC), mw, tm=tm_proj,
                    tiles_per_group=S // tm_proj)
        xp = _ffn(xp, mod_p(2), gn(2), w_up16, w_down16, l, 1, tm_p, S // tm_p)
        outs_p.append((
            jnp.transpose(kt32.reshape(B, HB, dh, S), (0, 3, 1, 2)),
            jnp.transpose(vt32.reshape(B, HB, dh, S), (0, 3, 1, 2)),
            jnp.transpose(sgt[:, :HB, :], (0, 2, 1)),
            jnp.swapaxes(st[:, :, :DV].reshape(B, HC, DK, DV), -1, -2),
            st[:, :, DV].reshape(B, HC, DK),
            mo[:, :, 0]))

        xs = _ffn(xs, mod_s(0), gn(0), w_up16, w_down16, l, 0, Ms, 1)
        (ya_s, q16_s, k32_s, v32_s, qc_s, kc_s, vc_s, sg_s, va_s) = _proj(
            xs, mod_s(1), gn(1), pw_s, prompt=False, batch=1, tm=Ms, chunk=Ms, groups=GA, dh=dh,
            n_lf=HB, n_ic=HC, emit_va=True)
        pad_t = lambda a: jnp.pad(a.reshape(Bd, T, -1), ((0, 0), (0, psz - T), (0, 0)))
        lft_new = jnp.swapaxes(pad_t(sg_s[:, :HB]), 1, 2)
        yb_s = _fox_sample(page_table, q16_s.astype(F32).reshape(Bd, T, DB), pad_t(k32_s).astype(BF16),
                           pad_t(v32_s).astype(BF16), lft_new, cache_kt, cache_vt, cache_lft, l,
                           n_heads=HB, dh=dh, pages=pages)
        padc = lambda a: jnp.pad(a.reshape(Bd, T, -1), ((0, 0), (0, MLSTM_CHUNK - T), (0, 0)))
        hc_s, st_s, mo_s = _mlstm(
            padc(qc_s), jnp.swapaxes(padc(kc_s), 1, 2), padc(vc_s), padc(sg_s),
            jnp.swapaxes(state_C[l], -1, -2).reshape(Bd, DQK, DV), state_n[l].reshape(Bd, DQK, 1),
            state_m[l].reshape(Bd, HC, 1), n_heads=HC, dk=DK, dv=DV, valid=T, ic_col=HB, lf_col=HB + HC)
        xs = _merge(xs, mod_s(1), gn(1), ya_s, yb_s.reshape(Ms, DB).astype(BF16), hc_s[:, :T].reshape(Ms, DC), mw,
                    tm=Ms, tiles_per_group=1)
        xs = _ffn(xs, mod_s(2), gn(2), w_up16, w_down16, l, 1, Ms, 1)
        outs_s.append((
            k32_s.reshape(Bd, T, HB, dh), v32_s.reshape(Bd, T, HB, dh), sg_s[:, :HB].reshape(Bd, T, HB),
            jnp.swapaxes(st_s[:, :, :DV].reshape(Bd, HC, DK, DV), -1, -2),
            st_s[:, :, DV].reshape(Bd, HC, DK),
            mo_s[:, :, 0]))
        chunk_v.append(va_s.reshape(Bd, T, DA))

    stk = lambda states, i: jnp.stack([s[i] for s in states])
    return (xp.reshape(B, S, D), xs.reshape(Bd, T, D),
            stk(outs_p, 0), stk(outs_p, 1), stk(outs_p, 2), stk(outs_p, 3), stk(outs_p, 4), stk(outs_p, 5),
            stk(outs_s, 0), stk(outs_s, 1), stk(outs_s, 2), stk(outs_s, 3), stk(outs_s, 4), stk(outs_s, 5),
            jnp.stack(chunk_v))
```

```python
import functools

import numpy as np
import jax
import jax.numpy as jnp
from jax import lax
from jax.experimental import pallas as pl
from jax.experimental.pallas import tpu as pltpu

F32 = jnp.float32
BF16 = jnp.bfloat16
EPS = 1e-6
FFN_RES = 0.5
LANES = 128
MLSTM_CHUNK = 128
VMEM_LIMIT = 56 * 1024 * 1024
NT_DIMS = (((1,), (1,)), ((), ()))


def _dot(a, b):
    return jnp.dot(a, b, preferred_element_type=F32)


def _dot_nt(a, b):
    return lax.dot_general(a, b, NT_DIMS, preferred_element_type=F32)


def _split3(x):
    hi = x.astype(BF16)
    r = x - hi.astype(F32)
    mid = r.astype(BF16)
    lo = (r - mid.astype(F32)).astype(BF16)
    return hi, mid, lo


def _dot_exact_l(a, x):
    hi, mid, lo = _split3(x)
    return _dot(a, hi) + _dot(a, mid) + _dot(a, lo)


def _dot_exact_r(x, a):
    hi, mid, lo = _split3(x)
    return _dot(hi, a) + _dot(mid, a) + _dot(lo, a)


def _modnorm(x, g, shift, scale):
    y = x * lax.rsqrt(jnp.mean(x * x, axis=-1, keepdims=True) + EPS)
    return (y * g) * (1 + scale) + shift


def _params(sem):
    return pltpu.CompilerParams(dimension_semantics=sem, vmem_limit_bytes=VMEM_LIMIT)


def _ada_kernel(c_ref, w_ref, b_ref, o_ref):
    s = jax.nn.silu(c_ref[...]).astype(BF16)
    o_ref[...] = _dot(s, w_ref[...].astype(BF16)) + b_ref[...]


def _ada_mod(c_all, w_ada, b_ada):
    depth, d, n = w_ada.shape
    r = c_all.shape[0]
    tn = n // 8
    return pl.pallas_call(
        _ada_kernel,
        grid=(depth, n // tn),
        in_specs=[pl.BlockSpec((r, d), lambda l, j: (0, 0)),
                  pl.BlockSpec((None, d, tn), lambda l, j: (l, 0, j)),
                  pl.BlockSpec((None, 1, tn), lambda l, j: (l, 0, j))],
        out_specs=pl.BlockSpec((None, r, tn), lambda l, j: (l, 0, j)),
        out_shape=jax.ShapeDtypeStruct((depth, r, n), F32),
        compiler_params=_params(("arbitrary", "arbitrary")),
        name="ada_mod",
    )(c_all, w_ada, b_ada.reshape(depth, 1, n))


def _ffn_kernel(x_ref, sh_ref, sc_ref, gt_ref, g_ref, wa_ref, wb_ref, wd_ref, o_ref, h_scr, acc_scr):
    j = pl.program_id(1)

    @pl.when(j == 0)
    def _():
        h = _modnorm(x_ref[...], g_ref[...], sh_ref[...], sc_ref[...])
        h_scr[...] = h.astype(BF16)
        acc_scr[...] = jnp.zeros_like(acc_scr)

    h = h_scr[...]
    a = _dot(h, wa_ref[...])
    b = _dot(h, wb_ref[...])
    act = (jax.nn.silu(a) * b).astype(BF16)
    acc_scr[...] += _dot(act, wd_ref[...])

    @pl.when(j == pl.num_programs(1) - 1)
    def _():
        o_ref[...] = x_ref[...] + (FFN_RES * gt_ref[...]) * acc_scr[...]


def _ffn(x, mod, g, w_up, w_down, l, k, tm, tiles_per_group):
    m, d = x.shape
    dff = w_down.shape[2]
    tf = dff // 2 if (dff // 2) % LANES == 0 else dff
    nj = dff // tf
    r = mod[0].shape[1]
    modspec = pl.BlockSpec((None, r, d), lambda i, j: (i // tiles_per_group, 0, 0))
    return pl.pallas_call(
        _ffn_kernel,
        grid=(m // tm, nj),
        in_specs=[pl.BlockSpec((tm, d), lambda i, j: (i, 0)), modspec, modspec, modspec,
                  pl.BlockSpec((1, d), lambda i, j: (0, 0)),
                  pl.BlockSpec((None, None, d, tf), lambda i, j: (l, k, 0, j)),
                  pl.BlockSpec((None, None, d, tf), lambda i, j: (l, k, 0, j + nj)),
                  pl.BlockSpec((None, None, tf, d), lambda i, j: (l, k, j, 0))],
        out_specs=pl.BlockSpec((tm, d), lambda i, j: (i, 0)),
        out_shape=jax.ShapeDtypeStruct((m, d), F32),
        scratch_shapes=[pltpu.VMEM((tm, d), BF16), pltpu.VMEM((tm, d), F32)],
        compiler_params=_params(("arbitrary", "arbitrary")),
        name="ffn",
    )(x, mod[0], mod[1], mod[2], g, w_up, w_up, w_down)


def _group_rms_lanes(z, gsz):
    n = z.shape[-1]
    zz = z * z
    parts = []
    lane = lax.broadcasted_iota(jnp.int32, (z.shape[0], LANES), 1)
    for p in range(n // LANES):
        blk = zz[:, p * LANES:(p + 1) * LANES]
        scale = jnp.zeros_like(blk)
        for r in range(LANES // gsz):
            sel = (lane >= r * gsz) & (lane < (r + 1) * gsz)
            ms = jnp.sum(jnp.where(sel, blk, 0.0), axis=-1, keepdims=True) / gsz
            scale = jnp.where(sel, lax.rsqrt(ms + EPS), scale)
        parts.append(scale)
    return jnp.concatenate(parts, axis=-1)


def _rms_rows(zt, dh, gcol):
    outs = []
    for hh in range(zt.shape[0] // dh):
        blk = zt[hh * dh:(hh + 1) * dh, :]
        ms = jnp.mean(blk * blk, axis=0, keepdims=True)
        outs.append((blk * lax.rsqrt(ms + EPS)) * gcol)
    return outs


BIAS_LANES_PER_HEAD = 6


def _proj_kernel(*refs, names, prompt, tm, chunk, groups, dh, n_lf, n_ic):
    r = dict(zip(names, refs))
    h = _modnorm(r["x"][...], r["g"][...], r["sh"][...], r["sc"][...]).astype(BF16)

    za = _dot(h, r["wa"][...])
    da = za.shape[1] // 2
    dg = da // groups
    ua = jax.nn.gelu(za[:, :da])
    vg = jax.nn.gelu(za[:, da:])
    va = (vg * lax.rsqrt(jnp.mean(vg * vg, axis=-1, keepdims=True) + EPS)) * r["gva"][...]
    if "va" in r:
        r["va"][...] = va
    vab = va.astype(BF16)
    rr = lax.broadcasted_iota(jnp.int32, (chunk, chunk), 0)
    cc = lax.broadcasted_iota(jnp.int32, (chunk, chunk), 1)
    for gi in range(groups):
        w = jnp.where(cc <= rr, r["wmix"][gi], 0.0).astype(BF16)
        bcol = r["bmix"][:, gi:gi + 1]
        for c in range(tm // chunk):
            rs = slice(c * chunk, (c + 1) * chunk)
            cs = slice(gi * dg, (gi + 1) * dg)
            mix = _dot(w, vab[rs, cs]) + bcol
            r["ya"][rs, cs] = (ua[rs, cs] * mix).astype(BF16)

    zs = _dot(h, r["ws"][...]) + r["bsg"][...]
    col = lax.broadcasted_iota(jnp.int32, zs.shape, 1)
    lsg = jnp.where((col >= n_lf) & (col < n_lf + n_ic), zs, jax.nn.log_sigmoid(zs))
    r["sg"][...] = lsg

    zc = _dot(h, r["wc"][...])
    dqk = r["qc"].shape[-1]
    dkc = dqk // n_ic
    r["qc"][...] = zc[:, :dqk]
    r["vc"][...] = zc[:, dqk:]

    scale = dh ** -0.5
    zk = _dot(h, r["wk"][...])
    kn = (zk * _group_rms_lanes(zk, dh)) * r["gk_row"][...]
    if not prompt:
        zq = _dot(h, r["wq"][...])
        r["q32"][...] = ((zq * _group_rms_lanes(zq, dh)) * r["gq_row"][...]) * scale
        r["k32"][...] = kn
        r["v32"][...] = _dot(h, r["wv"][...])
        r["kc"][...] = _dot(h, r["wkc"][...]) * (dkc ** -0.5)
        return

    for hh, blk in enumerate(_rms_rows(_dot_nt(r["wqt"][...], h), dh, r["gq_col"][...])):
        r["qt16"][hh * dh:(hh + 1) * dh, :] = (blk * scale).astype(BF16)
    for hh, blk in enumerate(_rms_rows(_dot_nt(r["wkt"][...], h), dh, r["gk_col"][...])):
        r["kt32"][hh * dh:(hh + 1) * dh, :] = blk
    vt = _dot_nt(r["wvt"][...], h)
    r["vt32"][...] = vt
    r["vt16"][...] = vt.astype(BF16)
    r["kct"][...] = _dot_nt(r["wkct"][...], h) * (dkc ** -0.5)

    carry_r, carry_c = r["carry_r"], r["carry_c"]

    @pl.when(pl.program_id(1) == 0)
    def _():
        carry_r[...] = jnp.zeros_like(carry_r)
        carry_c[...] = jnp.zeros_like(carry_c)

    r2 = lax.broadcasted_iota(jnp.int32, (tm, tm), 0)
    c2 = lax.broadcasted_iota(jnp.int32, (tm, tm), 1)
    tril = jnp.where(c2 <= r2, 1.0, 0.0).astype(BF16)
    fcum = _dot_exact_l(tril, lsg) + carry_r[0:1, :]
    carry_r[0:1, :] = fcum[tm - 1:tm, :]

    zst = _dot_nt(r["wst"][...], h) + r["bsgt"][...]
    row = lax.broadcasted_iota(jnp.int32, zst.shape, 0)
    lsgt = jnp.where((row >= n_lf) & (row < n_lf + n_ic), zst, jax.nn.log_sigmoid(zst))
    r["sgt"][...] = lsgt
    triu = jnp.where(r2 <= c2, 1.0, 0.0).astype(BF16)
    ftc = _dot_exact_r(lsgt, triu) + carry_c[:, 0:1]
    r["ftc"][...] = ftc
    carry_c[:, 0:1] = ftc[:, tm - 1:tm]

    g_all = _dot_exact_r(fcum, r["sel"][...])
    lane = lax.broadcasted_iota(jnp.int32, (tm, LANES), 1)
    bl = BIAS_LANES_PER_HEAD
    is_one = (lane < bl // 2) | ((lane >= bl) & (lane < bl + bl // 2))
    piece = [(lane == bl // 2 + i) | (lane == bl + bl // 2 + i) for i in range(3)]
    for p in range(kn.shape[1] // LANES):
        hi, mid, lo = (v.astype(F32) for v in _split3(g_all[:, p * LANES:(p + 1) * LANES]))
        fsel = jnp.where(piece[0], hi, jnp.where(piece[1], mid, lo))
        blk = jnp.where(is_one, 1.0, jnp.where(lane < 2 * bl, -fsel, 0.0))
        r["kaug"][:, 2 * p * LANES:(2 * p + 1) * LANES] = kn[:, p * LANES:(p + 1) * LANES].astype(BF16)
        r["kaug"][:, (2 * p + 1) * LANES:(2 * p + 2) * LANES] = blk.astype(BF16)


def _proj(x, mod, g, pw, *, prompt, batch, tm, chunk, groups, dh, n_lf, n_ic):
    m, d = x.shape
    s_len = m // batch
    nt = s_len // tm
    rmod = mod[0].shape[1]
    da = pw["wa"].shape[1] // 2
    db = pw["wk"].shape[1]
    dvc = pw["dvc"]
    dqk = pw["wc"].shape[1] - dvc
    nsg = pw["ws"].shape[1]

    def row(i, j):
        return (i * nt + j, 0)

    modspec = pl.BlockSpec((None, rmod, d), (lambda i, j: (i, 0, 0)) if rmod == 1 else (lambda i, j: (i * nt + j, 0, 0)))
    full = lambda a: pl.BlockSpec(a.shape, lambda i, j: (0,) * a.ndim)
    rowspec = lambda n: pl.BlockSpec((tm, n), row)
    colspec = lambda n: pl.BlockSpec((None, n, tm), lambda i, j: (i, 0, j))
    slabspec = lambda n: pl.BlockSpec((None, None, n, tm), lambda i, j: (i, j, 0, 0))

    ins = [("x", x, rowspec(d)), ("sh", mod[0], modspec), ("sc", mod[1], modspec), ("g", g, full(g))]
    wnames = (["wa", "wqt", "wk", "wkt", "wvt", "wc", "wkct", "ws", "wst", "gva", "gq_col", "gk_row", "gk_col",
               "bsg", "bsgt", "sel", "wmix", "bmix"] if prompt else
              ["wa", "wq", "wk", "wv", "wc", "wkc", "ws", "gva", "gq_row", "gk_row", "bsg", "wmix", "bmix"])
    ins += [(n, pw[n], full(pw[n])) for n in wnames]

    sds = jax.ShapeDtypeStruct
    if prompt:
        nsgt = pw["wst"].shape[0]
        outs = [("ya", sds((m, da), BF16), rowspec(da)),
                ("qt16", sds((batch, db, s_len), BF16), colspec(db)),
                ("kt32", sds((batch, db, s_len), F32), colspec(db)),
                ("kaug", sds((m, 2 * db), BF16), rowspec(2 * db)),
                ("vt32", sds((batch, db, s_len), F32), colspec(db)),
                ("vt16", sds((batch, nt, db, tm), BF16), slabspec(db)),
                ("qc", sds((m, dqk), F32), rowspec(dqk)),
                ("kct", sds((batch, dqk, s_len), F32), colspec(dqk)),
                ("vc", sds((m, dvc), F32), rowspec(dvc)),
                ("sg", sds((m, nsg), F32), rowspec(nsg)),
                ("sgt", sds((batch, nsgt, s_len), F32), colspec(nsgt)),
                ("ftc", sds((batch, nsgt, s_len), F32), colspec(nsgt))]
        scratch = [("carry_r", pltpu.VMEM((8, nsg), F32)), ("carry_c", pltpu.VMEM((nsgt, LANES), F32))]
    else:
        outs = [("ya", sds((m, da), BF16), rowspec(da)), ("q32", sds((m, db), F32), rowspec(db)),
                ("k32", sds((m, db), F32), rowspec(db)), ("v32", sds((m, db), F32), rowspec(db)),
                ("qc", sds((m, dqk), F32), rowspec(dqk)), ("kc", sds((m, dqk), F32), rowspec(dqk)),
                ("vc", sds((m, dvc), F32), rowspec(dvc)), ("sg", sds((m, nsg), F32), rowspec(nsg)),
                ("va", sds((m, da), F32), rowspec(da))]
        scratch = []
    names = tuple(n for n, _, _ in ins) + tuple(n for n, _, _ in outs) + tuple(n for n, _ in scratch)
    kern = functools.partial(_proj_kernel, names=names, prompt=prompt, tm=tm, chunk=chunk, groups=groups, dh=dh,
                             n_lf=n_lf, n_ic=n_ic)
    res = pl.pallas_call(
        kern, grid=(batch, nt), in_specs=[s for _, _, s in ins], out_specs=[s for _, _, s in outs],
        out_shape=[o for _, o, _ in outs], scratch_shapes=[s for _, s in scratch],
        compiler_params=_params(("arbitrary", "arbitrary")),
        name="proj_prompt" if prompt else "proj_sample",
    )(*[a for _, a, _ in ins])
    return dict(zip([n for n, _, _ in outs], res))


def _fox_prompt_kernel(qt_ref, fq_ref, ka_ref, vt_ref, o_ref, qa_scr, m_scr, l_scr, acc_scr, *, tq, tk, dh):
    qi = pl.program_id(1)
    n_pairs = qt_ref.shape[0] // LANES
    ratio = tq // tk
    bl = BIAS_LANES_PER_HEAD
    rowi = lax.broadcasted_iota(jnp.int32, (LANES, tq), 0)
    krow = lax.broadcasted_iota(jnp.int32, (tk, tq), 0)
    qcol = lax.broadcasted_iota(jnp.int32, (tk, tq), 1)

    for p in range(n_pairs):
        qpair = qt_ref[p * LANES:(p + 1) * LANES, :].astype(F32)
        for e in range(2):
            hi, mid, lo = (v.astype(F32) for v in _split3(fq_ref[2 * p + e:2 * p + e + 1, :]))
            top = jnp.where((rowi >= e * dh) & (rowi < (e + 1) * dh), qpair, 0.0)
            bot = jnp.where(rowi == bl * e, hi, jnp.where(rowi == bl * e + 1, mid, jnp.where(
                rowi == bl * e + 2, lo, jnp.where((rowi >= bl * e + 3) & (rowi < bl * e + 6), 1.0, 0.0))))
            qa_scr[e, 0:LANES, :] = top.astype(BF16)
            qa_scr[e, LANES:2 * LANES, :] = bot.astype(BF16)
        m_scr[...] = jnp.full_like(m_scr, -jnp.inf)
        l_scr[...] = jnp.zeros_like(l_scr)
        acc_scr[...] = jnp.zeros_like(acc_scr)

        def step(kj, diag_off):
            ks = ka_ref[pl.ds(pl.multiple_of(kj * tk, tk), tk), 2 * p * LANES:(2 * p + 2) * LANES]
            for e in range(2):
                st = _dot(ks, qa_scr[e])
                if diag_off is not None:
                    st = jnp.where(krow + diag_off * tk <= qcol, st, -jnp.inf)
                m_prev = m_scr[e]
                m_new = jnp.maximum(m_prev, jnp.max(st, axis=0, keepdims=True))
                alpha = jnp.exp(m_prev - m_new)
                pt = jnp.exp(st - m_new)
                l_scr[e] = alpha * l_scr[e] + jnp.sum(pt, axis=0, keepdims=True)
                vth = vt_ref[kj, (2 * p + e) * dh:(2 * p + e + 1) * dh, :]
                acc_scr[e] = alpha * acc_scr[e] + _dot(vth, pt.astype(BF16))
                m_scr[e] = m_new

        for d_off in range(ratio):
            step(qi * ratio + d_off, d_off)

        def body(kj, carry):
            step(kj, None)
            return carry

        lax.fori_loop(0, qi * ratio, body, 0)
        out_t = jnp.concatenate([acc_scr[e] / l_scr[e] for e in range(2)], axis=0)
        o_ref[:, p * LANES:(p + 1) * LANES] = out_t.T.astype(BF16)


def _fox_prompt(qt, ftc, kaug, vt, *, tq, dh):
    batch, db, s_len = qt.shape
    nk, tk = vt.shape[1], vt.shape[3]
    nq = s_len // tq
    kern = functools.partial(_fox_prompt_kernel, tq=tq, tk=tk, dh=dh)
    return pl.pallas_call(
        kern, grid=(batch, nq),
        in_specs=[pl.BlockSpec((None, db, tq), lambda b, i: (b, 0, i)),
                  pl.BlockSpec((None, ftc.shape[1], tq), lambda b, i: (b, 0, i)),
                  pl.BlockSpec((s_len, kaug.shape[1]), lambda b, i: (b, 0)),
                  pl.BlockSpec((None, nk, db, tk), lambda b, i: (b, 0, 0, 0))],
        out_specs=pl.BlockSpec((tq, db), lambda b, i: (b * nq + i, 0)),
        out_shape=jax.ShapeDtypeStruct((batch * s_len, db), BF16),
        scratch_shapes=[pltpu.VMEM((2, 2 * LANES, tq), BF16), pltpu.VMEM((2, 1, tq), F32),
                        pltpu.VMEM((2, 1, tq), F32), pltpu.VMEM((2, dh, tq), F32)],
        compiler_params=_params(("arbitrary", "arbitrary")),
        name="fox_prompt",
    )(qt, ftc, kaug, vt)


def _fox_sample_kernel(pt_ref, q_ref, kn_ref, vn_ref, lfn_ref, *refs, n_tok, n_heads, dh, pages):
    del pt_ref
    k_refs = refs[:pages]
    v_refs = refs[pages:2 * pages]
    lf_refs = refs[2 * pages:3 * pages]
    o_ref, qs_scr, m_scr, l_scr, acc_scr, r_scr, cq_scr = refs[3 * pages:]
    j = pl.program_id(1)
    nrow = n_tok * n_heads
    dbw = q_ref.shape[-1]
    psz = kn_ref.shape[0]
    hrow = lax.broadcasted_iota(jnp.int32, (n_heads, dbw), 0)
    hcol = lax.broadcasted_iota(jnp.int32, (n_heads, dbw), 1)
    headmask = (hcol >= hrow * dh) & (hcol < (hrow + 1) * dh)
    r2 = lax.broadcasted_iota(jnp.int32, (psz, psz), 0)
    c2 = lax.broadcasted_iota(jnp.int32, (psz, psz), 1)

    def update(s, pv_fn):
        m_prev = m_scr[...]
        m_new = jnp.maximum(m_prev, jnp.max(s, axis=-1, keepdims=True))
        alpha = jnp.exp(m_prev - m_new)
        pm = jnp.exp(s - m_new)
        l_scr[...] = alpha * l_scr[...] + jnp.sum(pm, axis=-1, keepdims=True)
        acc_scr[...] = alpha * acc_scr[...] + pv_fn(pm.astype(BF16))
        m_scr[...] = m_new

    @pl.when(j == 0)
    def _():
        q = q_ref[...]
        qs = jnp.concatenate(
            [jnp.where(headmask, jnp.broadcast_to(q[t:t + 1, :], (n_heads, dbw)), 0.0)
             for t in range(n_tok)], axis=0).astype(BF16)
        qs_scr[...] = qs
        triu = jnp.where(r2 <= c2, 1.0, 0.0).astype(BF16)
        cn = _dot_exact_r(lfn_ref[...], triu)
        cq = jnp.concatenate([cn[:, t:t + 1] for t in range(n_tok)], axis=0)
        cq_scr[...] = cq
        bias = cq - jnp.concatenate([cn] * n_tok, axis=0)
        s = _dot_nt(qs, kn_ref[...]) + bias
        trow = lax.broadcasted_iota(jnp.int32, (nrow, psz), 0) // n_heads
        scol = lax.broadcasted_iota(jnp.int32, (nrow, psz), 1)
        s = jnp.where(scol <= trow, s, -jnp.inf)
        m_scr[...] = jnp.full_like(m_scr, -jnp.inf)
        l_scr[...] = jnp.zeros_like(l_scr)
        acc_scr[...] = jnp.zeros_like(acc_scr)
        r_scr[...] = jnp.zeros_like(r_scr)
        update(s, lambda pm: _dot(pm, vn_ref[...]))

    qs = qs_scr[...]
    cq = cq_scr[...]
    later = jnp.where(r2 > c2, 1.0, 0.0).astype(BF16)
    lf_all = jnp.concatenate([lf_refs[i][...] for i in range(pages)], axis=0)
    suf_all = _dot_exact_r(lf_all, later)
    tot_all = jnp.sum(lf_all, axis=-1, keepdims=True)
    rsum = r_scr[:, 0:1]
    parts = []
    for i in range(pages):
        suf = suf_all[i * n_heads:(i + 1) * n_heads, :] + rsum
        parts.append(_dot(qs, k_refs[i][...].astype(BF16)) + (cq + jnp.concatenate([suf] * n_tok, axis=0)))
        rsum = rsum + tot_all[i * n_heads:(i + 1) * n_heads, :]
    r_scr[:, 0:1] = rsum

    def pv_pages(pm):
        acc = None
        for i in range(pages):
            term = _dot_nt(pm[:, i * psz:(i + 1) * psz], v_refs[i][...].astype(BF16))
            acc = term if acc is None else acc + term
        return acc

    update(jnp.concatenate(parts, axis=1), pv_pages)

    @pl.when(j == pl.num_programs(1) - 1)
    def _():
        out = acc_scr[...] / l_scr[...]
        for t in range(n_tok):
            blk = jnp.where(headmask, out[t * n_heads:(t + 1) * n_heads, :], 0.0)
            o_ref[t:t + 1, :] = jnp.sum(blk, axis=0, keepdims=True).astype(o_ref.dtype)


def _fox_sample(page_table, q, k_new, v_new, lft_new, cache_kt, cache_vt, cache_lft, layer, *, n_heads, dh, pages):
    bd, n_tok, dbw = q.shape
    n_pages = page_table.shape[1]
    psz = cache_kt.shape[-1]
    steps = n_pages // pages
    nrow = n_tok * n_heads

    def page_map(i):
        return lambda b, j, pt: (layer, pt[b, n_pages - 1 - (j * pages + i)], 0, 0)

    per_b = lambda shape: pl.BlockSpec((None,) + shape, lambda b, j, pt: (b, 0, 0))
    in_specs = [per_b((n_tok, dbw)), per_b((psz, dbw)), per_b((psz, dbw)), per_b((n_heads, psz))]
    in_specs += [pl.BlockSpec((None, None, dbw, psz), page_map(i)) for i in range(pages)]
    in_specs += [pl.BlockSpec((None, None, dbw, psz), page_map(i)) for i in range(pages)]
    in_specs += [pl.BlockSpec((None, None, n_heads, psz), page_map(i)) for i in range(pages)]
    kern = functools.partial(_fox_sample_kernel, n_tok=n_tok, n_heads=n_heads, dh=dh, pages=pages)
    grid_spec = pltpu.PrefetchScalarGridSpec(
        num_scalar_prefetch=1, grid=(bd, steps), in_specs=in_specs,
        out_specs=pl.BlockSpec((None, n_tok, dbw), lambda b, j, pt: (b, 0, 0)),
        scratch_shapes=[pltpu.VMEM((nrow, dbw), BF16), pltpu.VMEM((nrow, 1), F32), pltpu.VMEM((nrow, 1), F32),
                        pltpu.VMEM((nrow, dbw), F32), pltpu.VMEM((n_heads, LANES), F32),
                        pltpu.VMEM((nrow, 1), F32)])
    return pl.pallas_call(
        kern, grid_spec=grid_spec, out_shape=jax.ShapeDtypeStruct((bd, n_tok, dbw), F32),
        compiler_params=_params(("arbitrary", "arbitrary")), name="fox_sample",
    )(page_table, q, k_new, v_new, lft_new, *([cache_kt] * pages), *([cache_vt] * pages), *([cache_lft] * pages))


def _mlstm_kernel(q_ref, kt_ref, v_ref, sg_ref, c0_ref, n0_ref, m0_ref, h_ref, st_ref, mo_ref, st_scr, m_scr,
                  *, n_heads, dk, dv, valid, ic_col, lf_col):
    c = pl.program_id(1)
    L = q_ref.shape[0]
    dqk = n_heads * dk

    @pl.when(c == 0)
    def _():
        st_scr[:, :dv] = c0_ref[...]
        st_scr[:, dv:] = jnp.broadcast_to(n0_ref[...], (dqk, st_scr.shape[1] - dv))
        for hh in range(n_heads):
            m_scr[hh] = jnp.broadcast_to(m0_ref[hh:hh + 1, 0:1], (L, L))

    q = q_ref[...].astype(BF16)
    kt = kt_ref[...].astype(BF16)
    sg = sg_ref[...]
    row = lax.broadcasted_iota(jnp.int32, (L, L), 0)
    col = lax.broadcasted_iota(jnp.int32, (L, L), 1)
    tril = jnp.where(col <= row, 1.0, 0.0).astype(BF16)
    ones = jnp.ones((L, L), BF16)
    eye = col == row
    mask = (col <= row) & (col < valid)
    qlane = lax.broadcasted_iota(jnp.int32, (L, dqk), 1)
    st_bf = st_scr[...].astype(BF16)

    for hh in range(n_heads):
        qm = jnp.where((qlane >= hh * dk) & (qlane < (hh + 1) * dk), q, jnp.zeros_like(q))
        kth = kt[hh * dk:(hh + 1) * dk, :]
        vh = v_ref[:, hh * dv:(hh + 1) * dv]
        lf_c = jnp.broadcast_to(sg[:, lf_col + hh:lf_col + hh + 1], (L, L))
        ig_c = jnp.broadcast_to(sg[:, ic_col + hh:ic_col + hh + 1], (L, L))
        b_c = _dot_exact_l(tril, lf_c)
        a_c = ig_c - b_c
        a_r = _dot_exact_l(ones, jnp.where(eye, a_c, 0.0))
        m_old = m_scr[hh]
        dmat = jnp.where(mask, b_c + a_r, -jnp.inf)
        inter = b_c + m_old
        m_t = jnp.maximum(jnp.max(dmat, axis=-1, keepdims=True), inter)
        w_intra = jnp.exp(dmat - m_t) * _dot(qm, kt)
        w_inter = jnp.exp(inter - m_t)
        qcn = _dot(qm, st_bf)
        num = _dot(w_intra.astype(BF16), vh.astype(BF16)) + w_inter[:, 0:1] * qcn[:, :dv]
        den = jnp.sum(w_intra, axis=-1, keepdims=True) + w_inter[:, 0:1] * qcn[:, dv:dv + 1]
        hout = num / jnp.maximum(jnp.abs(den), jnp.exp(-m_t[:, 0:1]))
        h_ref[:, hh * dv:(hh + 1) * dv] = hout

        m_new = jnp.broadcast_to(m_t[valid - 1:valid, :], (L, L))
        b_last = jnp.broadcast_to(b_c[valid - 1:valid, :], (L, L))
        w_k = jnp.where(row < valid, jnp.exp(b_last + a_c - m_new), 0.0)
        decay = jnp.exp(b_last + m_old - m_new)[0:1, 0:1]
        rhs = jnp.concatenate([w_k[:, 0:1] * vh, jnp.broadcast_to(w_k[:, 0:1], (L, st_scr.shape[1] - dv))],
                              axis=1).astype(BF16)
        upd = _dot(kth, rhs)
        st_scr[hh * dk:(hh + 1) * dk, :] = decay * st_scr[hh * dk:(hh + 1) * dk, :] + upd
        m_scr[hh] = m_new

    @pl.when(c == pl.num_programs(1) - 1)
    def _():
        st_ref[...] = st_scr[...]
        for hh in range(n_heads):
            mo_ref[hh:hh + 1, :] = m_scr[hh][0:1, :]


def _mlstm(q, kt, v, sg, c0t, n0, m0, *, n_heads, dk, dv, valid, ic_col, lf_col):
    batch, s_len, dqk = q.shape
    L = MLSTM_CHUNK
    nc = s_len // L
    wst = 2 * dv
    kern = functools.partial(_mlstm_kernel, n_heads=n_heads, dk=dk, dv=dv, valid=valid, ic_col=ic_col, lf_col=lf_col)
    return pl.pallas_call(
        kern, grid=(batch, nc),
        in_specs=[pl.BlockSpec((None, L, dqk), lambda b, c: (b, c, 0)),
                  pl.BlockSpec((None, dqk, L), lambda b, c: (b, 0, c)),
                  pl.BlockSpec((None, L, n_heads * dv), lambda b, c: (b, c, 0)),
                  pl.BlockSpec((None, L, sg.shape[2]), lambda b, c: (b, c, 0)),
                  pl.BlockSpec((None, dqk, dv), lambda b, c: (b, 0, 0)),
                  pl.BlockSpec((None, dqk, 1), lambda b, c: (b, 0, 0)),
                  pl.BlockSpec((None, n_heads, 1), lambda b, c: (b, 0, 0))],
        out_specs=[pl.BlockSpec((None, L, n_heads * dv), lambda b, c: (b, c, 0)),
                   pl.BlockSpec((None, dqk, wst), lambda b, c: (b, 0, 0)),
                   pl.BlockSpec((None, n_heads, L), lambda b, c: (b, 0, 0))],
        out_shape=[jax.ShapeDtypeStruct((batch, s_len, n_heads * dv), F32),
                   jax.ShapeDtypeStruct((batch, dqk, wst), F32),
                   jax.ShapeDtypeStruct((batch, n_heads, L), F32)],
        scratch_shapes=[pltpu.VMEM((dqk, wst), F32), pltpu.VMEM((n_heads, L, L), F32)],
        compiler_params=_params(("arbitrary", "arbitrary")),
        name="mlstm",
    )(q, kt, v, sg, c0t, n0, m0)


def _merge_kernel(x_ref, sh_ref, sc_ref, gt_ref, g_ref, ya_ref, yb_ref, hc_ref,
                  woc_ref, wgt_ref, wbr_ref, wout_ref, o_ref):
    x = x_ref[...]
    h = _modnorm(x, g_ref[...], sh_ref[...], sc_ref[...]).astype(BF16)
    oc = jax.nn.sigmoid(_dot(h, woc_ref[...]))
    ys = (ya_ref[...], yb_ref[...], (oc * hc_ref[...]).astype(BF16))
    d = x.shape[1]
    acc = None
    for nb in range(len(ys)):
        gate = jax.nn.sigmoid(_dot(h, wgt_ref[:, nb * d:(nb + 1) * d]))
        term = gate * _dot(ys[nb], wbr_ref[nb])
        acc = term if acc is None else acc + term
    o_ref[...] = x + gt_ref[...] * _dot(acc.astype(BF16), wout_ref[...])


def _merge(x, mod, g, ya, yb, hc, mw, *, tm, tiles_per_group):
    m, d = x.shape
    r = mod[0].shape[1]
    modspec = pl.BlockSpec((None, r, d), lambda i: (i // tiles_per_group, 0, 0))
    full = lambda a: pl.BlockSpec(a.shape, lambda i: (0,) * a.ndim)
    rowspec = lambda n: pl.BlockSpec((tm, n), lambda i: (i, 0))
    ws = [mw["woc"], mw["wgt"], mw["wbr"], mw["wout"]]
    return pl.pallas_call(
        _merge_kernel, grid=(m // tm,),
        in_specs=[rowspec(d), modspec, modspec, modspec, full(g), rowspec(ya.shape[1]), rowspec(yb.shape[1]),
                  rowspec(hc.shape[1])] + [full(w) for w in ws],
        out_specs=rowspec(d),
        out_shape=jax.ShapeDtypeStruct((m, d), F32),
        compiler_params=_params(("arbitrary",)),
        name="merge",
    )(x, mod[0], mod[1], mod[2], g, ya, yb, hc, *ws)


def _pick_tile(n, pref):
    t = min(n, pref)
    while n % t:
        t //= 2
    return t


def kernel(x_prompt, x_sample, cache_k, cache_v, cache_logf, state_C, state_n, state_m, page_table,
           c_prompt, c_sample, w_ada, b_ada, g_norm, w_ff_up, w_ff_down, w_in, g_va, w_s, b_s,
           g_qb, g_kb, b_fb, b_ic, b_fc, w_branch, w_out):
    B, S, D = x_prompt.shape
    Bd, T, _ = x_sample.shape
    depth = w_ada.shape[0]
    n_pool, psz, HB, dh = cache_k.shape[1:]
    HC, DV, DK = state_C.shape[2:]
    GA, chunk_a = w_s.shape[1], w_s.shape[2]
    DA = g_va.shape[1]
    DB = HB * dh
    DQK = HC * DK
    DC = HC * DV
    NBR, BW = w_branch.shape[1], w_branch.shape[2]
    n_sub = g_norm.shape[1]
    Ms = Bd * T

    splits = (DA, DA, DB, DB, DB, HB, DQK, DQK, DC, HC, HC, DC, NBR * D)
    offs = [0]
    for sz in splits:
        offs.append(offs[-1] + sz)
    (o_ua, o_va, o_qb, o_kb, o_vb, o_fb, o_qc, o_kc, o_vc, o_ic, o_fc, o_oc, o_gt, o_end) = offs

    w_in_t = jnp.swapaxes(w_in, 1, 2)
    nsg = LANES
    nsgt = 2 * ((HB + 2 * HC + 15) // 16) * 8
    n_small = HB + 2 * HC

    def cols(l, a, b):
        return jnp.swapaxes(w_in_t[l, a:b], 0, 1).astype(BF16)

    def rows(l, a, b):
        return w_in_t[l, a:b].astype(BF16)

    w_up16 = w_ff_up.astype(BF16)
    w_down16 = w_ff_down.astype(BF16)
    w_br16 = w_branch.astype(BF16)
    w_out16 = w_out.astype(BF16)

    sel_np = np.zeros((nsg, (HB // 2) * LANES), np.float32)
    for hd in range(HB):
        base = (hd // 2) * LANES + (hd % 2) * BIAS_LANES_PER_HEAD + BIAS_LANES_PER_HEAD // 2
        sel_np[hd, base:base + 3] = 1.0
    sel = jnp.asarray(sel_np, BF16)

    layer_w = []
    for l in range(depth):
        small_rows = jnp.concatenate([w_in_t[l, o_fb:o_qc], w_in_t[l, o_ic:o_oc]], axis=0)
        wst = jnp.zeros((nsgt, D), F32).at[:n_small].set(small_rows).astype(BF16)
        ws = jnp.zeros((D, nsg), F32).at[:, :n_small].set(small_rows.T).astype(BF16)
        bias_small = jnp.concatenate([b_fb[l], b_ic[l], b_fc[l]])
        bsg = jnp.zeros((1, nsg), F32).at[0, :n_small].set(bias_small)
        bsgt = jnp.zeros((nsgt, 1), F32).at[:n_small, 0].set(bias_small)
        common = dict(
            wa=cols(l, o_ua, o_qb), wk=cols(l, o_kb, o_vb),
            wc=jnp.concatenate([cols(l, o_qc, o_kc), cols(l, o_vc, o_ic)], axis=1), dvc=DC,
            ws=ws, gva=g_va[l].reshape(1, DA), gk_row=jnp.tile(g_kb[l], HB).reshape(1, DB), bsg=bsg)
        wblk = jnp.einsum("ab,gts->gatbs", jnp.eye(Bd, dtype=F32), w_s[l][:, :T, :T]).reshape(GA, Ms, Ms)
        pw_p = dict(common, wqt=rows(l, o_qb, o_kb), wkt=rows(l, o_kb, o_vb), wvt=rows(l, o_vb, o_fb),
                    wkct=rows(l, o_kc, o_vc), wst=wst, gq_col=g_qb[l].reshape(dh, 1), gk_col=g_kb[l].reshape(dh, 1),
                    bsgt=bsgt, sel=sel, wmix=w_s[l], bmix=b_s[l].T)
        pw_s = dict(common, wq=cols(l, o_qb, o_kb), wv=cols(l, o_vb, o_fb), wkc=cols(l, o_kc, o_vc),
                    gq_row=jnp.tile(g_qb[l], HB).reshape(1, DB), wmix=wblk,
                    bmix=jnp.tile(b_s[l][:, :T], (1, Bd)).T)
        mw = dict(woc=cols(l, o_oc, o_gt), wgt=cols(l, o_gt, o_end), wbr=w_br16[l], wout=w_out16[l])
        layer_w.append((pw_p, pw_s, mw))

    r_all = B + Bd
    r_pad = -(-r_all // 8) * 8
    c_all = jnp.zeros((r_pad, D), F32).at[:B].set(c_prompt).at[B:r_all].set(c_sample)
    mods = _ada_mod(c_all, w_ada, b_ada).reshape(depth, r_pad, n_sub, 3, D)

    cache_kt = jnp.transpose(cache_k, (0, 1, 3, 4, 2)).reshape(depth, n_pool, DB, psz)
    cache_vt = jnp.transpose(cache_v, (0, 1, 3, 4, 2)).reshape(depth, n_pool, DB, psz)
    cache_lft = jnp.transpose(cache_logf, (0, 1, 3, 2))

    tm_p = _pick_tile(S, 512)
    tm_proj = _pick_tile(S, 256)
    tq = _pick_tile(S, 512)
    pages = _pick_tile(page_table.shape[1], 16)

    xp = x_prompt.reshape(B * S, D)
    xs = x_sample.reshape(Ms, D)
    outs_p, outs_s, chunk_v = [], [], []
    for l in range(depth):
        pw_p, pw_s, mw = layer_w[l]

        def mod_p(sub):
            return tuple(mods[l, :B, sub, i].reshape(B, 1, D) for i in range(3))

        def mod_s(sub):
            return tuple(jnp.repeat(mods[l, B:r_all, sub, i], T, axis=0).reshape(1, Ms, D) for i in range(3))

        gn = lambda sub: g_norm[l, sub].reshape(1, D)

        xp = _ffn(xp, mod_p(0), gn(0), w_up16, w_down16, l, 0, tm_p, S // tm_p)
        pr = _proj(xp, mod_p(1), gn(1), pw_p, prompt=True, batch=B, tm=tm_proj, chunk=chunk_a, groups=GA, dh=dh,
                   n_lf=HB, n_ic=HC)
        yb = _fox_prompt(pr["qt16"], pr["ftc"], pr["kaug"], pr["vt16"], tq=tq, dh=dh)
        hc, st, mo = _mlstm(pr["qc"].reshape(B, S, DQK), pr["kct"], pr["vc"].reshape(B, S, DC),
                            pr["sg"].reshape(B, S, nsg),
                            jnp.zeros((B, DQK, DV), F32), jnp.zeros((B, DQK, 1), F32), jnp.zeros((B, HC, 1), F32),
                            n_heads=HC, dk=DK, dv=DV, valid=MLSTM_CHUNK, ic_col=HB, lf_col=HB + HC)
        xp = _merge(xp, mod_p(1), gn(1), pr["ya"], yb, hc.reshape(B * S, DC), mw, tm=tm_proj,
                    tiles_per_group=S // tm_proj)
        xp = _ffn(xp, mod_p(2), gn(2), w_up16, w_down16, l, 1, tm_p, S // tm_p)
        outs_p.append((
            jnp.transpose(pr["kt32"].reshape(B, HB, dh, S), (0, 3, 1, 2)),
            jnp.transpose(pr["vt32"].reshape(B, HB, dh, S), (0, 3, 1, 2)),
            jnp.transpose(pr["sgt"][:, :HB, :], (0, 2, 1)),
            jnp.swapaxes(st[:, :, :DV].reshape(B, HC, DK, DV), -1, -2),
            st[:, :, DV].reshape(B, HC, DK),
            mo[:, :, 0]))

        xs = _ffn(xs, mod_s(0), gn(0), w_up16, w_down16, l, 0, Ms, 1)
        ps = _proj(xs, mod_s(1), gn(1), pw_s, prompt=False, batch=1, tm=Ms, chunk=Ms, groups=GA, dh=dh,
                   n_lf=HB, n_ic=HC)
        k32_s, v32_s, sg_s = ps["k32"], ps["v32"], ps["sg"]
        pad_t = lambda a: jnp.pad(a.reshape(Bd, T, -1), ((0, 0), (0, psz - T), (0, 0)))
        lft_new = jnp.swapaxes(pad_t(sg_s[:, :HB]), 1, 2)
        yb_s = _fox_sample(page_table, ps["q32"].reshape(Bd, T, DB), pad_t(k32_s).astype(BF16),
                           pad_t(v32_s).astype(BF16), lft_new, cache_kt, cache_vt, cache_lft, l,
                           n_heads=HB, dh=dh, pages=pages)
        padc = lambda a: jnp.pad(a.reshape(Bd, T, -1), ((0, 0), (0, MLSTM_CHUNK - T), (0, 0)))
        hc_s, st_s, mo_s = _mlstm(
            padc(ps["qc"]), jnp.swapaxes(padc(ps["kc"]), 1, 2), padc(ps["vc"]), padc(sg_s),
            jnp.swapaxes(state_C[l], -1, -2).reshape(Bd, DQK, DV), state_n[l].reshape(Bd, DQK, 1),
            state_m[l].reshape(Bd, HC, 1), n_heads=HC, dk=DK, dv=DV, valid=T, ic_col=HB, lf_col=HB + HC)
        xs = _merge(xs, mod_s(1), gn(1), ps["ya"], yb_s.reshape(Ms, DB).astype(BF16), hc_s[:, :T].reshape(Ms, DC),
                    mw, tm=Ms, tiles_per_group=1)
        xs = _ffn(xs, mod_s(2), gn(2), w_up16, w_down16, l, 1, Ms, 1)
        outs_s.append((
            k32_s.reshape(Bd, T, HB, dh), v32_s.reshape(Bd, T, HB, dh), sg_s[:, :HB].reshape(Bd, T, HB),
            jnp.swapaxes(st_s[:, :, :DV].reshape(Bd, HC, DK, DV), -1, -2),
            st_s[:, :, DV].reshape(Bd, HC, DK),
            mo_s[:, :, 0]))
        chunk_v.append(ps["va"].reshape(Bd, T, DA))

    stk = lambda states, i: jnp.stack([s[i] for s in states])
    return (xp.reshape(B, S, D), xs.reshape(Bd, T, D),
            stk(outs_p, 0), stk(outs_p, 1), stk(outs_p, 2), stk(outs_p, 3), stk(outs_p, 4), stk(outs_p, 5),
            stk(outs_s, 0), stk(outs_s, 1), stk(outs_s, 2), stk(outs_s, 3), stk(outs_s, 4), stk(outs_s, 5),
            jnp.stack(chunk_v))
```

```python
import functools

import numpy as np
import jax
import jax.numpy as jnp
from jax import lax
from jax.experimental import pallas as pl
from jax.experimental.pallas import tpu as pltpu

F32 = jnp.float32
BF16 = jnp.bfloat16
EPS = 1e-6
FFN_RES = 0.5
LANES = 128
MLSTM_CHUNK = 128
VMEM_LIMIT = 56 * 1024 * 1024
NT_DIMS = (((1,), (1,)), ((), ()))


def _dot(a, b):
    return jnp.dot(a, b, preferred_element_type=F32)


def _dot_nt(a, b):
    return lax.dot_general(a, b, NT_DIMS, preferred_element_type=F32)


def _split3(x):
    hi = x.astype(BF16)
    r = x - hi.astype(F32)
    mid = r.astype(BF16)
    lo = (r - mid.astype(F32)).astype(BF16)
    return hi, mid, lo


def _split2(x):
    hi = x.astype(BF16)
    return hi, (x - hi.astype(F32)).astype(BF16)


def _dot_hilo(a, b):
    return _dot(a[0], b[0]) + (_dot(a[0], b[1]) + _dot(a[1], b[0]))


def _dot_exact_l(a, x):
    hi, mid, lo = _split3(x)
    return _dot(a, hi) + _dot(a, mid) + _dot(a, lo)


def _dot_exact_r(x, a):
    hi, mid, lo = _split3(x)
    return _dot(hi, a) + _dot(mid, a) + _dot(lo, a)


def _modnorm(x, g, shift, scale):
    y = x * lax.rsqrt(jnp.mean(x * x, axis=-1, keepdims=True) + EPS)
    return (y * g) * (1 + scale) + shift


def _params(sem):
    return pltpu.CompilerParams(dimension_semantics=sem, vmem_limit_bytes=VMEM_LIMIT)


def _ada_kernel(c_ref, w_ref, b_ref, o_ref):
    s = jax.nn.silu(c_ref[...]).astype(BF16)
    o_ref[...] = _dot(s, w_ref[...].astype(BF16)) + b_ref[...]


def _ada_mod(c_all, w_ada, b_ada):
    depth, d, n = w_ada.shape
    r = c_all.shape[0]
    tn = n // 8
    return pl.pallas_call(
        _ada_kernel,
        grid=(depth, n // tn),
        in_specs=[pl.BlockSpec((r, d), lambda l, j: (0, 0)),
                  pl.BlockSpec((None, d, tn), lambda l, j: (l, 0, j)),
                  pl.BlockSpec((None, 1, tn), lambda l, j: (l, 0, j))],
        out_specs=pl.BlockSpec((None, r, tn), lambda l, j: (l, 0, j)),
        out_shape=jax.ShapeDtypeStruct((depth, r, n), F32),
        compiler_params=_params(("arbitrary", "arbitrary")),
        name="ada_mod",
    )(c_all, w_ada, b_ada.reshape(depth, 1, n))


def _ffn_kernel(x_ref, sh_ref, sc_ref, gt_ref, g_ref, wu_ref, wd_ref, o_ref, *, n_chunks):
    x = x_ref[...]
    h = _modnorm(x, g_ref[...], sh_ref[...], sc_ref[...]).astype(BF16)
    dff = wd_ref.shape[0]
    tf = dff // n_chunks
    acc = None
    for j in range(n_chunks):
        a = _dot(h, wu_ref[:, j * tf:(j + 1) * tf])
        b = _dot(h, wu_ref[:, dff + j * tf:dff + (j + 1) * tf])
        act = (jax.nn.silu(a) * b).astype(BF16)
        term = _dot(act, wd_ref[j * tf:(j + 1) * tf, :])
        acc = term if acc is None else acc + term
    o_ref[...] = x + (FFN_RES * gt_ref[...]) * acc


def _ffn(x, mod, g, w_up, w_down, l, k, tm, tiles_per_group):
    m, d = x.shape
    dff = w_down.shape[2]
    n_chunks = 2 if (dff // 2) % LANES == 0 else 1
    r = mod[0].shape[1]
    modspec = pl.BlockSpec((None, r, d), lambda i: (i // tiles_per_group, 0, 0))
    resident = lambda shape: pl.BlockSpec((None, None) + shape, lambda i: (l, k, 0, 0), pipeline_mode=pl.Buffered(1))
    return pl.pallas_call(
        functools.partial(_ffn_kernel, n_chunks=n_chunks),
        grid=(m // tm,),
        in_specs=[pl.BlockSpec((tm, d), lambda i: (i, 0)), modspec, modspec, modspec,
                  pl.BlockSpec((1, d), lambda i: (0, 0)), resident((d, 2 * dff)), resident((dff, d))],
        out_specs=pl.BlockSpec((tm, d), lambda i: (i, 0)),
        out_shape=jax.ShapeDtypeStruct((m, d), F32),
        compiler_params=_params(("arbitrary",)),
        name="ffn",
    )(x, mod[0], mod[1], mod[2], g, w_up, w_down)


def _group_rms_lanes(z, gsz):
    n = z.shape[-1]
    zz = z * z
    parts = []
    lane = lax.broadcasted_iota(jnp.int32, (z.shape[0], LANES), 1)
    for p in range(n // LANES):
        blk = zz[:, p * LANES:(p + 1) * LANES]
        scale = jnp.zeros_like(blk)
        for r in range(LANES // gsz):
            sel = (lane >= r * gsz) & (lane < (r + 1) * gsz)
            ms = jnp.sum(jnp.where(sel, blk, 0.0), axis=-1, keepdims=True) / gsz
            scale = jnp.where(sel, lax.rsqrt(ms + EPS), scale)
        parts.append(scale)
    return jnp.concatenate(parts, axis=-1)


def _rms_rows(zt, dh, gcol):
    outs = []
    for hh in range(zt.shape[0] // dh):
        blk = zt[hh * dh:(hh + 1) * dh, :]
        ms = jnp.mean(blk * blk, axis=0, keepdims=True)
        outs.append((blk * lax.rsqrt(ms + EPS)) * gcol)
    return outs


BIAS_LANES_PER_HEAD = 6


def _proj_kernel(*refs, names, prompt, tm, tk, chunk, groups, dh, n_lf, n_ic):
    r = dict(zip(names, refs))
    h = _modnorm(r["x"][...], r["g"][...], r["sh"][...], r["sc"][...]).astype(BF16)

    za = _dot(h, r["wa"][...])
    da = za.shape[1] // 2
    dg = da // groups
    ua = jax.nn.gelu(za[:, :da])
    vg = jax.nn.gelu(za[:, da:])
    va = (vg * lax.rsqrt(jnp.mean(vg * vg, axis=-1, keepdims=True) + EPS)) * r["gva"][...]
    if "va" in r:
        r["va"][...] = va
    vab = va.astype(BF16)
    rr = lax.broadcasted_iota(jnp.int32, (chunk, chunk), 0)
    cc = lax.broadcasted_iota(jnp.int32, (chunk, chunk), 1)
    for gi in range(groups):
        w = jnp.where(cc <= rr, r["wmix"][gi], 0.0).astype(BF16)
        bcol = r["bmix"][:, gi:gi + 1]
        for c in range(tm // chunk):
            rs = slice(c * chunk, (c + 1) * chunk)
            cs = slice(gi * dg, (gi + 1) * dg)
            mix = _dot(w, vab[rs, cs]) + bcol
            r["ya"][rs, cs] = (ua[rs, cs] * mix).astype(BF16)

    zs = _dot(h, r["ws"][...]) + r["bsg"][...]
    col = lax.broadcasted_iota(jnp.int32, zs.shape, 1)
    lsg = jnp.where((col >= n_lf) & (col < n_lf + n_ic), zs, jax.nn.log_sigmoid(zs))
    r["sg"][...] = lsg

    zc = _dot(h, r["wc"][...])
    dqk = r["qc"].shape[-1]
    dkc = dqk // n_ic
    r["qc"][...] = zc[:, :dqk]
    r["vc"][...] = zc[:, dqk:]

    scale = dh ** -0.5
    zk = _dot(h, r["wk"][...])
    kn = (zk * _group_rms_lanes(zk, dh)) * r["gk_row"][...]
    if not prompt:
        zq = _dot(h, r["wq"][...])
        r["q32"][...] = ((zq * _group_rms_lanes(zq, dh)) * r["gq_row"][...]) * scale
        r["k32"][...] = kn
        r["v32"][...] = _dot(h, r["wv"][...])
        r["kc"][...] = _dot(h, r["wkc"][...]) * (dkc ** -0.5)
        return

    for hh, blk in enumerate(_rms_rows(_dot_nt(r["wqt"][...], h), dh, r["gq_col"][...])):
        r["qt16"][hh * dh:(hh + 1) * dh, :] = (blk * scale).astype(BF16)
    for hh, blk in enumerate(_rms_rows(_dot_nt(r["wkt"][...], h), dh, r["gk_col"][...])):
        r["kt32"][hh * dh:(hh + 1) * dh, :] = blk
    vt = _dot_nt(r["wvt"][...], h)
    r["vt32"][...] = vt
    for s in range(tm // tk):
        r["vt16"][s] = vt[:, s * tk:(s + 1) * tk].astype(BF16)
    r["kct"][...] = _dot_nt(r["wkct"][...], h) * (dkc ** -0.5)

    carry_r, carry_c = r["carry_r"], r["carry_c"]

    @pl.when(pl.program_id(1) == 0)
    def _():
        carry_r[...] = jnp.zeros_like(carry_r)
        carry_c[...] = jnp.zeros_like(carry_c)

    r2 = lax.broadcasted_iota(jnp.int32, (tm, tm), 0)
    c2 = lax.broadcasted_iota(jnp.int32, (tm, tm), 1)
    tril = jnp.where(c2 <= r2, 1.0, 0.0).astype(BF16)
    fcum = _dot_exact_l(tril, lsg) + carry_r[0:1, :]
    carry_r[0:1, :] = fcum[tm - 1:tm, :]

    zst = _dot_nt(r["wst"][...], h) + r["bsgt"][...]
    row = lax.broadcasted_iota(jnp.int32, zst.shape, 0)
    lsgt = jnp.where((row >= n_lf) & (row < n_lf + n_ic), zst, jax.nn.log_sigmoid(zst))
    r["sgt"][...] = lsgt
    triu = jnp.where(r2 <= c2, 1.0, 0.0).astype(BF16)
    ftc = _dot_exact_r(lsgt, triu) + carry_c[:, 0:1]
    r["ftc"][...] = ftc
    carry_c[:, 0:1] = ftc[:, tm - 1:tm]

    g_all = _dot_exact_r(fcum, r["sel"][...])
    lane = lax.broadcasted_iota(jnp.int32, (tm, LANES), 1)
    bl = BIAS_LANES_PER_HEAD
    is_one = (lane < bl // 2) | ((lane >= bl) & (lane < bl + bl // 2))
    piece = [(lane == bl // 2 + i) | (lane == bl + bl // 2 + i) for i in range(3)]
    for p in range(kn.shape[1] // LANES):
        hi, mid, lo = (v.astype(F32) for v in _split3(g_all[:, p * LANES:(p + 1) * LANES]))
        fsel = jnp.where(piece[0], hi, jnp.where(piece[1], mid, lo))
        blk = jnp.where(is_one, 1.0, jnp.where(lane < 2 * bl, -fsel, 0.0))
        r["kaug"][:, 2 * p * LANES:(2 * p + 1) * LANES] = kn[:, p * LANES:(p + 1) * LANES].astype(BF16)
        r["kaug"][:, (2 * p + 1) * LANES:(2 * p + 2) * LANES] = blk.astype(BF16)


def _proj(x, mod, g, pw, *, prompt, batch, tm, tk, chunk, groups, dh, n_lf, n_ic):
    m, d = x.shape
    s_len = m // batch
    nt = s_len // tm
    rmod = mod[0].shape[1]
    da = pw["wa"].shape[1] // 2
    db = pw["wk"].shape[1]
    dvc = pw["dvc"]
    dqk = pw["wc"].shape[1] - dvc
    nsg = pw["ws"].shape[1]

    def row(i, j):
        return (i * nt + j, 0)

    modspec = pl.BlockSpec((None, rmod, d), (lambda i, j: (i, 0, 0)) if rmod == 1 else (lambda i, j: (i * nt + j, 0, 0)))
    full = lambda a: pl.BlockSpec(a.shape, lambda i, j: (0,) * a.ndim, pipeline_mode=pl.Buffered(1))
    rowspec = lambda n: pl.BlockSpec((tm, n), row)
    colspec = lambda n: pl.BlockSpec((None, n, tm), lambda i, j: (i, 0, j))
    slabspec = lambda n: pl.BlockSpec((None, tm // tk, n, tk), lambda i, j: (i, j, 0, 0))

    ins = [("x", x, rowspec(d)), ("sh", mod[0], modspec), ("sc", mod[1], modspec), ("g", g, full(g))]
    wnames = (["wa", "wqt", "wk", "wkt", "wvt", "wc", "wkct", "ws", "wst", "gva", "gq_col", "gk_row", "gk_col",
               "bsg", "bsgt", "sel", "wmix", "bmix"] if prompt else
              ["wa", "wq", "wk", "wv", "wc", "wkc", "ws", "gva", "gq_row", "gk_row", "bsg", "wmix", "bmix"])
    ins += [(n, pw[n], full(pw[n])) for n in wnames]

    sds = jax.ShapeDtypeStruct
    if prompt:
        nsgt = pw["wst"].shape[0]
        outs = [("ya", sds((m, da), BF16), rowspec(da)),
                ("qt16", sds((batch, db, s_len), BF16), colspec(db)),
                ("kt32", sds((batch, db, s_len), F32), colspec(db)),
                ("kaug", sds((m, 2 * db), BF16), rowspec(2 * db)),
                ("vt32", sds((batch, db, s_len), F32), colspec(db)),
                ("vt16", sds((batch, s_len // tk, db, tk), BF16), slabspec(db)),
                ("qc", sds((m, dqk), F32), rowspec(dqk)),
                ("kct", sds((batch, dqk, s_len), F32), colspec(dqk)),
                ("vc", sds((m, dvc), F32), rowspec(dvc)),
                ("sg", sds((m, nsg), F32), rowspec(nsg)),
                ("sgt", sds((batch, nsgt, s_len), F32), colspec(nsgt)),
                ("ftc", sds((batch, nsgt, s_len), F32), colspec(nsgt))]
        scratch = [("carry_r", pltpu.VMEM((8, nsg), F32)), ("carry_c", pltpu.VMEM((nsgt, LANES), F32))]
    else:
        outs = [("ya", sds((m, da), BF16), rowspec(da)), ("q32", sds((m, db), F32), rowspec(db)),
                ("k32", sds((m, db), F32), rowspec(db)), ("v32", sds((m, db), F32), rowspec(db)),
                ("qc", sds((m, dqk), F32), rowspec(dqk)), ("kc", sds((m, dqk), F32), rowspec(dqk)),
                ("vc", sds((m, dvc), F32), rowspec(dvc)), ("sg", sds((m, nsg), F32), rowspec(nsg)),
                ("va", sds((m, da), F32), rowspec(da))]
        scratch = []
    names = tuple(n for n, _, _ in ins) + tuple(n for n, _, _ in outs) + tuple(n for n, _ in scratch)
    kern = functools.partial(_proj_kernel, names=names, prompt=prompt, tm=tm, tk=tk, chunk=chunk, groups=groups,
                             dh=dh, n_lf=n_lf, n_ic=n_ic)
    res = pl.pallas_call(
        kern, grid=(batch, nt), in_specs=[s for _, _, s in ins], out_specs=[s for _, _, s in outs],
        out_shape=[o for _, o, _ in outs], scratch_shapes=[s for _, s in scratch],
        compiler_params=_params(("arbitrary", "arbitrary")),
        name="proj_prompt" if prompt else "proj_sample",
    )(*[a for _, a, _ in ins])
    return dict(zip([n for n, _, _ in outs], res))


def _fox_prompt_kernel(qt_ref, fq_ref, ka_ref, vt_ref, o_ref, qa_scr, st_scr, m_scr, l_scr, acc_scr, *, tq, tk, dh):
    qi = pl.program_id(1)
    n_pairs = qt_ref.shape[0] // LANES
    assert tq == 2 * tk, "the key-block schedule below pairs blocks: two key blocks per query block"
    bl = BIAS_LANES_PER_HEAD
    rowi = lax.broadcasted_iota(jnp.int32, (LANES, tq), 0)
    krow = lax.broadcasted_iota(jnp.int32, (tk, tq), 0)
    qcol = lax.broadcasted_iota(jnp.int32, (tk, tq), 1)

    for p in range(n_pairs):
        qpair = qt_ref[p * LANES:(p + 1) * LANES, :].astype(F32)
        for e in range(2):
            hi, mid, lo = (v.astype(F32) for v in _split3(fq_ref[2 * p + e:2 * p + e + 1, :]))
            top = jnp.where((rowi >= e * dh) & (rowi < (e + 1) * dh), qpair, 0.0)
            bot = jnp.where(rowi == bl * e, hi, jnp.where(rowi == bl * e + 1, mid, jnp.where(
                rowi == bl * e + 2, lo, jnp.where((rowi >= bl * e + 3) & (rowi < bl * e + 6), 1.0, 0.0))))
            qa_scr[e, 0:LANES, :] = top.astype(BF16)
            qa_scr[e, LANES:2 * LANES, :] = bot.astype(BF16)
        m_scr[...] = jnp.full_like(m_scr, -jnp.inf)
        l_scr[...] = jnp.zeros_like(l_scr)
        acc_scr[...] = jnp.zeros_like(acc_scr)

        def scores(kj, slot):
            ks = ka_ref[pl.ds(pl.multiple_of(kj * tk, tk), tk), 2 * p * LANES:(2 * p + 2) * LANES]
            for e in range(2):
                st_scr[slot, e] = _dot(ks, qa_scr[e])

        def absorb(kj, slot, diag_off):
            for e in range(2):
                st = st_scr[slot, e]
                if diag_off is not None:
                    st = jnp.where(krow + diag_off * tk <= qcol, st, -jnp.inf)
                m_prev = m_scr[e]
                m_new = jnp.maximum(m_prev, jnp.max(st, axis=0, keepdims=True))
                alpha = jnp.exp(m_prev - m_new)
                pt = jnp.exp(st - m_new)
                l_scr[e] = alpha * l_scr[e] + jnp.sum(pt, axis=0, keepdims=True)
                vth = vt_ref[kj, (2 * p + e) * dh:(2 * p + e + 1) * dh, :]
                acc_scr[e] = alpha * acc_scr[e] + _dot(vth, pt.astype(BF16))
                m_scr[e] = m_new

        scores(0, 0)

        def body(jj, carry):
            scores(2 * jj + 1, 1)
            absorb(2 * jj, 0, None)
            scores(2 * jj + 2, 0)
            absorb(2 * jj + 1, 1, None)
            return carry

        lax.fori_loop(0, qi, body, 0)
        scores(2 * qi + 1, 1)
        absorb(2 * qi, 0, 0)
        absorb(2 * qi + 1, 1, 1)
        out_t = jnp.concatenate([acc_scr[e] / l_scr[e] for e in range(2)], axis=0)
        o_ref[:, p * LANES:(p + 1) * LANES] = out_t.T.astype(BF16)


def _fox_prompt(qt, ftc, kaug, vt, *, tq, dh):
    batch, db, s_len = qt.shape
    nk, tk = vt.shape[1], vt.shape[3]
    nq = s_len // tq
    kern = functools.partial(_fox_prompt_kernel, tq=tq, tk=tk, dh=dh)
    return pl.pallas_call(
        kern, grid=(batch, nq),
        in_specs=[pl.BlockSpec((None, db, tq), lambda b, i: (b, 0, i)),
                  pl.BlockSpec((None, ftc.shape[1], tq), lambda b, i: (b, 0, i)),
                  pl.BlockSpec((s_len, kaug.shape[1]), lambda b, i: (b, 0)),
                  pl.BlockSpec((None, nk, db, tk), lambda b, i: (b, 0, 0, 0))],
        out_specs=pl.BlockSpec((tq, db), lambda b, i: (b * nq + i, 0)),
        out_shape=jax.ShapeDtypeStruct((batch * s_len, db), BF16),
        scratch_shapes=[pltpu.VMEM((2, 2 * LANES, tq), BF16), pltpu.VMEM((2, 2, tk, tq), F32),
                        pltpu.VMEM((2, 1, tq), F32),
                        pltpu.VMEM((2, 1, tq), F32), pltpu.VMEM((2, dh, tq), F32)],
        compiler_params=_params(("arbitrary", "arbitrary")),
        name="fox_prompt",
    )(qt, ftc, kaug, vt)


def _fox_sample_kernel(pt_ref, q_ref, kn_ref, vn_ref, lfn_ref, *refs, n_tok, n_heads, dh, pages):
    del pt_ref
    k_refs = refs[:pages]
    v_refs = refs[pages:2 * pages]
    lf_refs = refs[2 * pages:3 * pages]
    o_ref, qs_scr, m_scr, l_scr, acc_scr, r_scr, cq_scr = refs[3 * pages:]
    j = pl.program_id(1)
    nrow = n_tok * n_heads
    dbw = q_ref.shape[-1]
    psz = kn_ref.shape[0]
    hrow = lax.broadcasted_iota(jnp.int32, (n_heads, dbw), 0)
    hcol = lax.broadcasted_iota(jnp.int32, (n_heads, dbw), 1)
    headmask = (hcol >= hrow * dh) & (hcol < (hrow + 1) * dh)
    r2 = lax.broadcasted_iota(jnp.int32, (psz, psz), 0)
    c2 = lax.broadcasted_iota(jnp.int32, (psz, psz), 1)

    def update(s, pv_fn):
        m_prev = m_scr[...]
        m_new = jnp.maximum(m_prev, jnp.max(s, axis=-1, keepdims=True))
        alpha = jnp.exp(m_prev - m_new)
        pm = jnp.exp(s - m_new)
        l_scr[...] = alpha * l_scr[...] + jnp.sum(pm, axis=-1, keepdims=True)
        acc_scr[...] = alpha * acc_scr[...] + pv_fn(pm.astype(BF16))
        m_scr[...] = m_new

    @pl.when(j == 0)
    def _():
        q = q_ref[...]
        qs = jnp.concatenate(
            [jnp.where(headmask, jnp.broadcast_to(q[t:t + 1, :], (n_heads, dbw)), 0.0)
             for t in range(n_tok)], axis=0).astype(BF16)
        qs_scr[...] = qs
        triu = jnp.where(r2 <= c2, 1.0, 0.0).astype(BF16)
        cn = _dot_exact_r(lfn_ref[...], triu)
        cq = jnp.concatenate([cn[:, t:t + 1] for t in range(n_tok)], axis=0)
        cq_scr[...] = cq
        bias = cq - jnp.concatenate([cn] * n_tok, axis=0)
        s = _dot_nt(qs, kn_ref[...]) + bias
        trow = lax.broadcasted_iota(jnp.int32, (nrow, psz), 0) // n_heads
        scol = lax.broadcasted_iota(jnp.int32, (nrow, psz), 1)
        s = jnp.where(scol <= trow, s, -jnp.inf)
        m_scr[...] = jnp.full_like(m_scr, -jnp.inf)
        l_scr[...] = jnp.zeros_like(l_scr)
        acc_scr[...] = jnp.zeros_like(acc_scr)
        r_scr[...] = jnp.zeros_like(r_scr)
        update(s, lambda pm: _dot(pm, vn_ref[...]))

    qs = qs_scr[...]
    cq = cq_scr[...]
    later = jnp.where(r2 > c2, 1.0, 0.0).astype(BF16)
    lf_all = jnp.concatenate([lf_refs[i][...] for i in range(pages)], axis=0)
    suf_all = _dot_exact_r(lf_all, later)
    tot_all = jnp.sum(lf_all, axis=-1, keepdims=True)
    rsum = r_scr[:, 0:1]
    parts = []
    for i in range(pages):
        suf = suf_all[i * n_heads:(i + 1) * n_heads, :] + rsum
        parts.append(_dot(qs, k_refs[i][...].astype(BF16)) + (cq + jnp.concatenate([suf] * n_tok, axis=0)))
        rsum = rsum + tot_all[i * n_heads:(i + 1) * n_heads, :]
    r_scr[:, 0:1] = rsum

    def pv_pages(pm):
        acc = None
        for i in range(pages):
            term = _dot_nt(pm[:, i * psz:(i + 1) * psz], v_refs[i][...].astype(BF16))
            acc = term if acc is None else acc + term
        return acc

    update(jnp.concatenate(parts, axis=1), pv_pages)

    @pl.when(j == pl.num_programs(1) - 1)
    def _():
        out = acc_scr[...] / l_scr[...]
        for t in range(n_tok):
            blk = jnp.where(headmask, out[t * n_heads:(t + 1) * n_heads, :], 0.0)
            o_ref[t:t + 1, :] = jnp.sum(blk, axis=0, keepdims=True).astype(o_ref.dtype)


def _fox_sample(page_table, q, k_new, v_new, lft_new, cache_kt, cache_vt, cache_lft, layer, *, n_heads, dh, pages):
    bd, n_tok, dbw = q.shape
    n_pages = page_table.shape[1]
    psz = cache_kt.shape[-1]
    steps = n_pages // pages
    nrow = n_tok * n_heads

    def page_map(i):
        return lambda b, j, pt: (layer, pt[b, n_pages - 1 - (j * pages + i)], 0, 0)

    per_b = lambda shape: pl.BlockSpec((None,) + shape, lambda b, j, pt: (b, 0, 0))
    in_specs = [per_b((n_tok, dbw)), per_b((psz, dbw)), per_b((psz, dbw)), per_b((n_heads, psz))]
    in_specs += [pl.BlockSpec((None, None, dbw, psz), page_map(i)) for i in range(pages)]
    in_specs += [pl.BlockSpec((None, None, dbw, psz), page_map(i)) for i in range(pages)]
    in_specs += [pl.BlockSpec((None, None, n_heads, psz), page_map(i)) for i in range(pages)]
    kern = functools.partial(_fox_sample_kernel, n_tok=n_tok, n_heads=n_heads, dh=dh, pages=pages)
    grid_spec = pltpu.PrefetchScalarGridSpec(
        num_scalar_prefetch=1, grid=(bd, steps), in_specs=in_specs,
        out_specs=pl.BlockSpec((None, n_tok, dbw), lambda b, j, pt: (b, 0, 0)),
        scratch_shapes=[pltpu.VMEM((nrow, dbw), BF16), pltpu.VMEM((nrow, 1), F32), pltpu.VMEM((nrow, 1), F32),
                        pltpu.VMEM((nrow, dbw), F32), pltpu.VMEM((n_heads, LANES), F32),
                        pltpu.VMEM((nrow, 1), F32)])
    return pl.pallas_call(
        kern, grid_spec=grid_spec, out_shape=jax.ShapeDtypeStruct((bd, n_tok, dbw), F32),
        compiler_params=_params(("arbitrary", "arbitrary")), name="fox_sample",
    )(page_table, q, k_new, v_new, lft_new, *([cache_kt] * pages), *([cache_vt] * pages), *([cache_lft] * pages))


def _mlstm_kernel(q_ref, kt_ref, v_ref, sg_ref, c0_ref, n0_ref, m0_ref, h_ref, st_ref, mo_ref, st_scr, m_scr,
                  *, n_heads, dk, dv, valid, ic_col, lf_col):
    c = pl.program_id(1)
    L = q_ref.shape[0]
    dqk = n_heads * dk

    @pl.when(c == 0)
    def _():
        st_scr[:, :dv] = c0_ref[...]
        st_scr[:, dv:] = jnp.broadcast_to(n0_ref[...], (dqk, st_scr.shape[1] - dv))
        for hh in range(n_heads):
            m_scr[hh] = jnp.broadcast_to(m0_ref[hh:hh + 1, 0:1], (L, L))

    q = q_ref[...]
    kt = _split2(kt_ref[...])
    sg = sg_ref[...]
    row = lax.broadcasted_iota(jnp.int32, (L, L), 0)
    col = lax.broadcasted_iota(jnp.int32, (L, L), 1)
    tril = jnp.where(col <= row, 1.0, 0.0).astype(BF16)
    ones = jnp.ones((L, L), BF16)
    eye = col == row
    mask = (col <= row) & (col < valid)
    qlane = lax.broadcasted_iota(jnp.int32, (L, dqk), 1)
    st_pair = _split2(st_scr[...])

    wide = lambda f: jnp.concatenate([f(hh) for hh in range(n_heads)], axis=1)
    lf_w = wide(lambda hh: jnp.broadcast_to(sg[:, lf_col + hh:lf_col + hh + 1], (L, L)))
    ig_w = wide(lambda hh: jnp.broadcast_to(sg[:, ic_col + hh:ic_col + hh + 1], (L, L)))
    b_w = _dot_exact_l(tril, lf_w)
    a_w = ig_w - b_w
    a_rw = _dot_exact_l(ones, jnp.where(wide(lambda hh: eye), ig_w, 0.0)
                        - jnp.where(wide(lambda hh: col >= row), lf_w, 0.0))
    qms = [_split2(jnp.where((qlane >= hh * dk) & (qlane < (hh + 1) * dk), q, 0.0)) for hh in range(n_heads)]
    qk_all = [_dot_hilo(qm, kt) for qm in qms]
    qcn_all = [_dot_hilo(qm, st_pair) for qm in qms]

    for hh in range(n_heads):
        kth = tuple(part[hh * dk:(hh + 1) * dk, :] for part in kt)
        vh = v_ref[:, hh * dv:(hh + 1) * dv]
        b_c = b_w[:, hh * L:(hh + 1) * L]
        a_c = a_w[:, hh * L:(hh + 1) * L]
        a_r = a_rw[:, hh * L:(hh + 1) * L]
        qcn = qcn_all[hh]
        m_old = m_scr[hh]
        dmat = jnp.where(mask, b_c + a_r, -jnp.inf)
        inter = b_c + m_old
        m_t = jnp.maximum(jnp.max(dmat, axis=-1, keepdims=True), inter)
        w_intra = jnp.exp(dmat - m_t) * qk_all[hh]
        w_inter = jnp.exp(inter - m_t)
        num = _dot_hilo(_split2(w_intra), _split2(vh)) + w_inter[:, 0:1] * qcn[:, :dv]
        den = jnp.sum(w_intra, axis=-1, keepdims=True) + w_inter[:, 0:1] * qcn[:, dv:dv + 1]
        hout = num / jnp.maximum(jnp.abs(den), jnp.exp(-m_t[:, 0:1]))
        h_ref[:, hh * dv:(hh + 1) * dv] = hout

        m_new = jnp.broadcast_to(m_t[valid - 1:valid, :], (L, L))
        b_last = jnp.broadcast_to(b_c[valid - 1:valid, :], (L, L))
        w_k = jnp.where(row < valid, jnp.exp(b_last + a_c - m_new), 0.0)
        decay = jnp.exp(b_last + m_old - m_new)[0:1, 0:1]
        rhs = jnp.concatenate([w_k[:, 0:1] * vh, jnp.broadcast_to(w_k[:, 0:1], (L, st_scr.shape[1] - dv))],
                              axis=1)
        upd = _dot_hilo(kth, _split2(rhs))
        st_scr[hh * dk:(hh + 1) * dk, :] = decay * st_scr[hh * dk:(hh + 1) * dk, :] + upd
        m_scr[hh] = m_new

    @pl.when(c == pl.num_programs(1) - 1)
    def _():
        st_ref[...] = st_scr[...]
        for hh in range(n_heads):
            mo_ref[hh:hh + 1, :] = m_scr[hh][0:1, :]


def _mlstm(q, kt, v, sg, c0t, n0, m0, *, n_heads, dk, dv, valid, ic_col, lf_col):
    batch, s_len, dqk = q.shape
    L = MLSTM_CHUNK
    nc = s_len // L
    wst = 2 * dv
    kern = functools.partial(_mlstm_kernel, n_heads=n_heads, dk=dk, dv=dv, valid=valid, ic_col=ic_col, lf_col=lf_col)
    return pl.pallas_call(
        kern, grid=(batch, nc),
        in_specs=[pl.BlockSpec((None, L, dqk), lambda b, c: (b, c, 0)),
                  pl.BlockSpec((None, dqk, L), lambda b, c: (b, 0, c)),
                  pl.BlockSpec((None, L, n_heads * dv), lambda b, c: (b, c, 0)),
                  pl.BlockSpec((None, L, sg.shape[2]), lambda b, c: (b, c, 0)),
                  pl.BlockSpec((None, dqk, dv), lambda b, c: (b, 0, 0)),
                  pl.BlockSpec((None, dqk, 1), lambda b, c: (b, 0, 0)),
                  pl.BlockSpec((None, n_heads, 1), lambda b, c: (b, 0, 0))],
        out_specs=[pl.BlockSpec((None, L, n_heads * dv), lambda b, c: (b, c, 0)),
                   pl.BlockSpec((None, dqk, wst), lambda b, c: (b, 0, 0)),
                   pl.BlockSpec((None, n_heads, L), lambda b, c: (b, 0, 0))],
        out_shape=[jax.ShapeDtypeStruct((batch, s_len, n_heads * dv), F32),
                   jax.ShapeDtypeStruct((batch, dqk, wst), F32),
                   jax.ShapeDtypeStruct((batch, n_heads, L), F32)],
        scratch_shapes=[pltpu.VMEM((dqk, wst), F32), pltpu.VMEM((n_heads, L, L), F32)],
        compiler_params=_params(("arbitrary", "arbitrary")),
        name="mlstm",
    )(q, kt, v, sg, c0t, n0, m0)


def _merge_kernel(x_ref, sh_ref, sc_ref, gt_ref, g_ref, ya_ref, yb_ref, hc_ref,
                  woc_ref, wgt_ref, wbr_ref, wout_ref, o_ref):
    x = x_ref[...]
    h = _modnorm(x, g_ref[...], sh_ref[...], sc_ref[...]).astype(BF16)
    oc = jax.nn.sigmoid(_dot(h, woc_ref[...]))
    ys = (ya_ref[...], yb_ref[...], (oc * hc_ref[...]).astype(BF16))
    d = x.shape[1]
    acc = None
    for nb in range(len(ys)):
        gate = jax.nn.sigmoid(_dot(h, wgt_ref[:, nb * d:(nb + 1) * d]))
        term = gate * _dot(ys[nb], wbr_ref[nb])
        acc = term if acc is None else acc + term
    o_ref[...] = x + gt_ref[...] * _dot(acc.astype(BF16), wout_ref[...])


def _merge(x, mod, g, ya, yb, hc, mw, *, tm, tiles_per_group):
    m, d = x.shape
    r = mod[0].shape[1]
    modspec = pl.BlockSpec((None, r, d), lambda i: (i // tiles_per_group, 0, 0))
    full = lambda a: pl.BlockSpec(a.shape, lambda i: (0,) * a.ndim, pipeline_mode=pl.Buffered(1))
    rowspec = lambda n: pl.BlockSpec((tm, n), lambda i: (i, 0))
    ws = [mw["woc"], mw["wgt"], mw["wbr"], mw["wout"]]
    return pl.pallas_call(
        _merge_kernel, grid=(m // tm,),
        in_specs=[rowspec(d), modspec, modspec, modspec, full(g), rowspec(ya.shape[1]), rowspec(yb.shape[1]),
                  rowspec(hc.shape[1])] + [full(w) for w in ws],
        out_specs=rowspec(d),
        out_shape=jax.ShapeDtypeStruct((m, d), F32),
        compiler_params=_params(("arbitrary",)),
        name="merge",
    )(x, mod[0], mod[1], mod[2], g, ya, yb, hc, *ws)


def _pick_tile(n, pref):
    t = min(n, pref)
    while n % t:
        t //= 2
    return t


def kernel(x_prompt, x_sample, cache_k, cache_v, cache_logf, state_C, state_n, state_m, page_table,
           c_prompt, c_sample, w_ada, b_ada, g_norm, w_ff_up, w_ff_down, w_in, g_va, w_s, b_s,
           g_qb, g_kb, b_fb, b_ic, b_fc, w_branch, w_out):
    B, S, D = x_prompt.shape
    Bd, T, _ = x_sample.shape
    depth = w_ada.shape[0]
    n_pool, psz, HB, dh = cache_k.shape[1:]
    HC, DV, DK = state_C.shape[2:]
    GA, chunk_a = w_s.shape[1], w_s.shape[2]
    DA = g_va.shape[1]
    DB = HB * dh
    DQK = HC * DK
    DC = HC * DV
    NBR, BW = w_branch.shape[1], w_branch.shape[2]
    n_sub = g_norm.shape[1]
    Ms = Bd * T

    splits = (DA, DA, DB, DB, DB, HB, DQK, DQK, DC, HC, HC, DC, NBR * D)
    offs = [0]
    for sz in splits:
        offs.append(offs[-1] + sz)
    (o_ua, o_va, o_qb, o_kb, o_vb, o_fb, o_qc, o_kc, o_vc, o_ic, o_fc, o_oc, o_gt, o_end) = offs

    w_in_t = jnp.swapaxes(w_in, 1, 2)
    nsg = LANES
    nsgt = 2 * ((HB + 2 * HC + 15) // 16) * 8
    n_small = HB + 2 * HC

    def cols(l, a, b):
        return jnp.swapaxes(w_in_t[l, a:b], 0, 1).astype(BF16)

    def rows(l, a, b):
        return w_in_t[l, a:b].astype(BF16)

    w_up16 = w_ff_up.astype(BF16)
    w_down16 = w_ff_down.astype(BF16)
    w_br16 = w_branch.astype(BF16)
    w_out16 = w_out.astype(BF16)

    sel_np = np.zeros((nsg, (HB // 2) * LANES), np.float32)
    for hd in range(HB):
        base = (hd // 2) * LANES + (hd % 2) * BIAS_LANES_PER_HEAD + BIAS_LANES_PER_HEAD // 2
        sel_np[hd, base:base + 3] = 1.0
    sel = jnp.asarray(sel_np, BF16)

    layer_w = []
    for l in range(depth):
        small_rows = jnp.concatenate([w_in_t[l, o_fb:o_qc], w_in_t[l, o_ic:o_oc]], axis=0)
        wst = jnp.zeros((nsgt, D), F32).at[:n_small].set(small_rows).astype(BF16)
        ws = jnp.zeros((D, nsg), F32).at[:, :n_small].set(small_rows.T).astype(BF16)
        bias_small = jnp.concatenate([b_fb[l], b_ic[l], b_fc[l]])
        bsg = jnp.zeros((1, nsg), F32).at[0, :n_small].set(bias_small)
        bsgt = jnp.zeros((nsgt, 1), F32).at[:n_small, 0].set(bias_small)
        common = dict(
            wa=cols(l, o_ua, o_qb), wk=cols(l, o_kb, o_vb),
            wc=jnp.concatenate([cols(l, o_qc, o_kc), cols(l, o_vc, o_ic)], axis=1), dvc=DC,
            ws=ws, gva=g_va[l].reshape(1, DA), gk_row=jnp.tile(g_kb[l], HB).reshape(1, DB), bsg=bsg)
        wblk = jnp.einsum("ab,gts->gatbs", jnp.eye(Bd, dtype=F32), w_s[l][:, :T, :T]).reshape(GA, Ms, Ms)
        pw_p = dict(common, wqt=rows(l, o_qb, o_kb), wkt=rows(l, o_kb, o_vb), wvt=rows(l, o_vb, o_fb),
                    wkct=rows(l, o_kc, o_vc), wst=wst, gq_col=g_qb[l].reshape(dh, 1), gk_col=g_kb[l].reshape(dh, 1),
                    bsgt=bsgt, sel=sel, wmix=w_s[l], bmix=b_s[l].T)
        pw_s = dict(common, wq=cols(l, o_qb, o_kb), wv=cols(l, o_vb, o_fb), wkc=cols(l, o_kc, o_vc),
                    gq_row=jnp.tile(g_qb[l], HB).reshape(1, DB), wmix=wblk,
                    bmix=jnp.tile(b_s[l][:, :T], (1, Bd)).T)
        mw = dict(woc=cols(l, o_oc, o_gt), wgt=cols(l, o_gt, o_end), wbr=w_br16[l], wout=w_out16[l])
        layer_w.append((pw_p, pw_s, mw))

    r_all = B + Bd
    r_pad = -(-r_all // 8) * 8
    c_all = jnp.zeros((r_pad, D), F32).at[:B].set(c_prompt).at[B:r_all].set(c_sample)
    mods = _ada_mod(c_all, w_ada, b_ada).reshape(depth, r_pad, n_sub, 3, D)

    cache_kt = jnp.transpose(cache_k, (0, 1, 3, 4, 2)).reshape(depth, n_pool, DB, psz)
    cache_vt = jnp.transpose(cache_v, (0, 1, 3, 4, 2)).reshape(depth, n_pool, DB, psz)
    cache_lft = jnp.transpose(cache_logf, (0, 1, 3, 2))

    tm_p = _pick_tile(S, 512)
    tm_proj = _pick_tile(S, 512)
    tk = _pick_tile(S, 256)
    tq = 2 * tk
    pages = _pick_tile(page_table.shape[1], 16)

    xp = x_prompt.reshape(B * S, D)
    xs = x_sample.reshape(Ms, D)
    outs_p, outs_s, chunk_v = [], [], []
    for l in range(depth):
        pw_p, pw_s, mw = layer_w[l]

        def mod_p(sub):
            return tuple(mods[l, :B, sub, i].reshape(B, 1, D) for i in range(3))

        def mod_s(sub):
            return tuple(jnp.repeat(mods[l, B:r_all, sub, i], T, axis=0).reshape(1, Ms, D) for i in range(3))

        gn = lambda sub: g_norm[l, sub].reshape(1, D)

        xp = _ffn(xp, mod_p(0), gn(0), w_up16, w_down16, l, 0, tm_p, S // tm_p)
        pr = _proj(xp, mod_p(1), gn(1), pw_p, prompt=True, batch=B, tm=tm_proj, tk=tk, chunk=chunk_a, groups=GA,
                   dh=dh, n_lf=HB, n_ic=HC)
        yb = _fox_prompt(pr["qt16"], pr["ftc"], pr["kaug"], pr["vt16"], tq=tq, dh=dh)
        hc, st, mo = _mlstm(pr["qc"].reshape(B, S, DQK), pr["kct"], pr["vc"].reshape(B, S, DC),
                            pr["sg"].reshape(B, S, nsg),
                            jnp.zeros((B, DQK, DV), F32), jnp.zeros((B, DQK, 1), F32), jnp.zeros((B, HC, 1), F32),
                            n_heads=HC, dk=DK, dv=DV, valid=MLSTM_CHUNK, ic_col=HB, lf_col=HB + HC)
        xp = _merge(xp, mod_p(1), gn(1), pr["ya"], yb, hc.reshape(B * S, DC), mw, tm=tm_proj,
                    tiles_per_group=S // tm_proj)
        xp = _ffn(xp, mod_p(2), gn(2), w_up16, w_down16, l, 1, tm_p, S // tm_p)
        outs_p.append((
            jnp.transpose(pr["kt32"].reshape(B, HB, dh, S), (0, 3, 1, 2)),
            jnp.transpose(pr["vt32"].reshape(B, HB, dh, S), (0, 3, 1, 2)),
            jnp.transpose(pr["sgt"][:, :HB, :], (0, 2, 1)),
            jnp.swapaxes(st[:, :, :DV].reshape(B, HC, DK, DV), -1, -2),
            st[:, :, DV].reshape(B, HC, DK),
            mo[:, :, 0]))

        xs = _ffn(xs, mod_s(0), gn(0), w_up16, w_down16, l, 0, Ms, 1)
        ps = _proj(xs, mod_s(1), gn(1), pw_s, prompt=False, batch=1, tm=Ms, tk=Ms, chunk=Ms, groups=GA, dh=dh,
                   n_lf=HB, n_ic=HC)
        k32_s, v32_s, sg_s = ps["k32"], ps["v32"], ps["sg"]
        pad_t = lambda a: jnp.pad(a.reshape(Bd, T, -1), ((0, 0), (0, psz - T), (0, 0)))
        lft_new = jnp.swapaxes(pad_t(sg_s[:, :HB]), 1, 2)
        yb_s = _fox_sample(page_table, ps["q32"].reshape(Bd, T, DB), pad_t(k32_s).astype(BF16),
                           pad_t(v32_s).astype(BF16), lft_new, cache_kt, cache_vt, cache_lft, l,
                           n_heads=HB, dh=dh, pages=pages)
        padc = lambda a: jnp.pad(a.reshape(Bd, T, -1), ((0, 0), (0, MLSTM_CHUNK - T), (0, 0)))
        hc_s, st_s, mo_s = _mlstm(
            padc(ps["qc"]), jnp.swapaxes(padc(ps["kc"]), 1, 2), padc(ps["vc"]), padc(sg_s),
            jnp.swapaxes(state_C[l], -1, -2).reshape(Bd, DQK, DV), state_n[l].reshape(Bd, DQK, 1),
            state_m[l].reshape(Bd, HC, 1), n_heads=HC, dk=DK, dv=DV, valid=T, ic_col=HB, lf_col=HB + HC)
        xs = _merge(xs, mod_s(1), gn(1), ps["ya"], yb_s.reshape(Ms, DB).astype(BF16), hc_s[:, :T].reshape(Ms, DC),
                    mw, tm=Ms, tiles_per_group=1)
        xs = _ffn(xs, mod_s(2), gn(2), w_up16, w_down16, l, 1, Ms, 1)
        outs_s.append((
            k32_s.reshape(Bd, T, HB, dh), v32_s.reshape(Bd, T, HB, dh), sg_s[:, :HB].reshape(Bd, T, HB),
            jnp.swapaxes(st_s[:, :, :DV].reshape(Bd, HC, DK, DV), -1, -2),
            st_s[:, :, DV].reshape(Bd, HC, DK),
            mo_s[:, :, 0]))
        chunk_v.append(ps["va"].reshape(Bd, T, DA))

    stk = lambda states, i: jnp.stack([s[i] for s in states])
    return (xp.reshape(B, S, D), xs.reshape(Bd, T, D),
            stk(outs_p, 0), stk(outs_p, 1), stk(outs_p, 2), stk(outs_p, 3), stk(outs_p, 4), stk(outs_p, 5),
            stk(outs_s, 0), stk(outs_s, 1), stk(outs_s, 2), stk(outs_s, 3), stk(outs_s, 4), stk(outs_s, 5),
            jnp.stack(chunk_v))
```

```python
import functools

import numpy as np
import jax
import jax.numpy as jnp
from jax import lax
from jax.experimental import pallas as pl
from jax.experimental.pallas import tpu as pltpu

F32 = jnp.float32
BF16 = jnp.bfloat16
EPS = 1e-6
FFN_RES = 0.5
LANES = 128
MLSTM_CHUNK = 128
VMEM_LIMIT = 56 * 1024 * 1024
NT_DIMS = (((1,), (1,)), ((), ()))


def _dot(a, b):
    return jnp.dot(a, b, preferred_element_type=F32)


def _dot_nt(a, b):
    return lax.dot_general(a, b, NT_DIMS, preferred_element_type=F32)


def _split3(x):
    hi = x.astype(BF16)
    r = x - hi.astype(F32)
    mid = r.astype(BF16)
    lo = (r - mid.astype(F32)).astype(BF16)
    return hi, mid, lo


def _split2(x):
    hi = x.astype(BF16)
    return hi, (x - hi.astype(F32)).astype(BF16)


def _dot_hilo(a, b):
    return _dot(a[0], b[0]) + (_dot(a[0], b[1]) + _dot(a[1], b[0]))


def _dot_exact_l(a, x):
    hi, mid, lo = _split3(x)
    return _dot(a, hi) + _dot(a, mid) + _dot(a, lo)


def _dot_exact_r(x, a):
    hi, mid, lo = _split3(x)
    return _dot(hi, a) + _dot(mid, a) + _dot(lo, a)


def _modnorm(x, g, shift, scale):
    y = x * lax.rsqrt(jnp.mean(x * x, axis=-1, keepdims=True) + EPS)
    return (y * g) * (1 + scale) + shift


def _params(sem):
    return pltpu.CompilerParams(dimension_semantics=sem, vmem_limit_bytes=VMEM_LIMIT)


def _ada_kernel(c_ref, w_ref, b_ref, o_ref):
    s = jax.nn.silu(c_ref[...]).astype(BF16)
    o_ref[...] = _dot(s, w_ref[...].astype(BF16)) + b_ref[...]


def _ada_mod(c_all, w_ada, b_ada):
    depth, d, n = w_ada.shape
    r = c_all.shape[0]
    tn = n // 8
    return pl.pallas_call(
        _ada_kernel,
        grid=(depth, n // tn),
        in_specs=[pl.BlockSpec((r, d), lambda l, j: (0, 0)),
                  pl.BlockSpec((None, d, tn), lambda l, j: (l, 0, j)),
                  pl.BlockSpec((None, 1, tn), lambda l, j: (l, 0, j))],
        out_specs=pl.BlockSpec((None, r, tn), lambda l, j: (l, 0, j)),
        out_shape=jax.ShapeDtypeStruct((depth, r, n), F32),
        compiler_params=_params(("arbitrary", "arbitrary")),
        name="ada_mod",
    )(c_all, w_ada, b_ada.reshape(depth, 1, n))


def _mod_specs(mod, tm, tile_index):
    arr, gain, l, sub = mod
    d = arr.shape[-1]
    if arr.ndim == 6:
        mspec = pl.BlockSpec((None, None, None, 3, 1, d), lambda *g: (l, sub, tile_index(*g)[0], 0, 0, 0))
    else:
        mspec = pl.BlockSpec((None, None, 3, tm, d), lambda *g: (l, sub, 0, tile_index(*g)[1], 0))
    return mspec, pl.BlockSpec((None, None, 1, d), lambda *g: (l, sub, 0, 0))


def _ffn_kernel(x_ref, mod_ref, g_ref, wu_ref, wd_ref, o_ref, *, n_chunks):
    x = x_ref[...]
    h = _modnorm(x, g_ref[...], mod_ref[0], mod_ref[1]).astype(BF16)
    dff = wd_ref.shape[0]
    tf = dff // n_chunks
    acc = None
    for j in range(n_chunks):
        a = _dot(h, wu_ref[:, j * tf:(j + 1) * tf])
        b = _dot(h, wu_ref[:, dff + j * tf:dff + (j + 1) * tf])
        act = (jax.nn.silu(a) * b).astype(BF16)
        term = _dot(act, wd_ref[j * tf:(j + 1) * tf, :])
        acc = term if acc is None else acc + term
    o_ref[...] = x + (FFN_RES * mod_ref[2]) * acc


def _ffn(x, mod, w_up, w_down, l, k, tm, tiles_per_group):
    m, d = x.shape
    dff = w_down.shape[2]
    n_chunks = 2 if (dff // 2) % LANES == 0 else 1
    modspec, gspec = _mod_specs(mod, tm, lambda i: (i // tiles_per_group, i))
    resident = lambda shape: pl.BlockSpec((None, None) + shape, lambda i: (l, k, 0, 0), pipeline_mode=pl.Buffered(1))
    return pl.pallas_call(
        functools.partial(_ffn_kernel, n_chunks=n_chunks),
        grid=(m // tm,),
        in_specs=[pl.BlockSpec((tm, d), lambda i: (i, 0)), modspec, gspec,
                  resident((d, 2 * dff)), resident((dff, d))],
        out_specs=pl.BlockSpec((tm, d), lambda i: (i, 0)),
        out_shape=jax.ShapeDtypeStruct((m, d), F32),
        compiler_params=_params(("arbitrary",)),
        name="ffn",
    )(x, mod[0], mod[1], w_up, w_down)


def _group_rms_lanes(z, gsz):
    n = z.shape[-1]
    zz = z * z
    parts = []
    lane = lax.broadcasted_iota(jnp.int32, (z.shape[0], LANES), 1)
    for p in range(n // LANES):
        blk = zz[:, p * LANES:(p + 1) * LANES]
        scale = jnp.zeros_like(blk)
        for r in range(LANES // gsz):
            sel = (lane >= r * gsz) & (lane < (r + 1) * gsz)
            ms = jnp.sum(jnp.where(sel, blk, 0.0), axis=-1, keepdims=True) / gsz
            scale = jnp.where(sel, lax.rsqrt(ms + EPS), scale)
        parts.append(scale)
    return jnp.concatenate(parts, axis=-1)


def _rms_rows(zt, dh, gcol):
    outs = []
    for hh in range(zt.shape[0] // dh):
        blk = zt[hh * dh:(hh + 1) * dh, :]
        ms = jnp.mean(blk * blk, axis=0, keepdims=True)
        outs.append((blk * lax.rsqrt(ms + EPS)) * gcol)
    return outs


BIAS_LANES_PER_HEAD = 6
STACKED_OUTPUTS = ("kt32", "vt32", "sgt")


def _proj_kernel(*refs, names, prompt, tm, tk, chunk, groups, dh, n_lf, n_ic):
    r = dict(zip(names, refs))
    h = _modnorm(r["x"][...], r["g"][...], r["mod"][0], r["mod"][1]).astype(BF16)

    za = _dot(h, r["wa"][...])
    da = za.shape[1] // 2
    dg = da // groups
    ua = jax.nn.gelu(za[:, :da])
    vg = jax.nn.gelu(za[:, da:])
    va = (vg * lax.rsqrt(jnp.mean(vg * vg, axis=-1, keepdims=True) + EPS)) * r["gva"][...]
    if "va" in r:
        r["va"][...] = va
    vab = va.astype(BF16)
    rr = lax.broadcasted_iota(jnp.int32, (chunk, chunk), 0)
    cc = lax.broadcasted_iota(jnp.int32, (chunk, chunk), 1)
    for gi in range(groups):
        w = jnp.where(cc <= rr, r["wmix"][gi], 0.0).astype(BF16)
        bcol = r["bmix"][:, gi:gi + 1]
        for c in range(tm // chunk):
            rs = slice(c * chunk, (c + 1) * chunk)
            cs = slice(gi * dg, (gi + 1) * dg)
            mix = _dot(w, vab[rs, cs]) + bcol
            r["ya"][rs, cs] = (ua[rs, cs] * mix).astype(BF16)

    zs = _dot(h, r["ws"][...]) + r["bsg"][...]
    col = lax.broadcasted_iota(jnp.int32, zs.shape, 1)
    lsg = jnp.where((col >= n_lf) & (col < n_lf + n_ic), zs, jax.nn.log_sigmoid(zs))
    r["sg"][...] = lsg

    zc = _dot(h, r["wc"][...])
    dqk = r["qc"].shape[-1]
    dkc = dqk // n_ic
    r["qc"][...] = zc[:, :dqk]
    r["vc"][...] = zc[:, dqk:]

    scale = dh ** -0.5
    zk = _dot(h, r["wk"][...])
    kn = (zk * _group_rms_lanes(zk, dh)) * r["gk_row"][...]
    if not prompt:
        zq = _dot(h, r["wq"][...])
        r["q32"][...] = ((zq * _group_rms_lanes(zq, dh)) * r["gq_row"][...]) * scale
        r["k32"][...] = kn
        r["v32"][...] = _dot(h, r["wv"][...])
        r["kc"][...] = _dot(h, r["wkc"][...]) * (dkc ** -0.5)
        return

    for hh, blk in enumerate(_rms_rows(_dot_nt(r["wqt"][...], h), dh, r["gq_col"][...])):
        r["qt16"][hh * dh:(hh + 1) * dh, :] = (blk * scale).astype(BF16)
    for hh, blk in enumerate(_rms_rows(_dot_nt(r["wkt"][...], h), dh, r["gk_col"][...])):
        r["kt32"][hh * dh:(hh + 1) * dh, :] = blk
    vt = _dot_nt(r["wvt"][...], h)
    r["vt32"][...] = vt
    for s in range(tm // tk):
        r["vt16"][s] = vt[:, s * tk:(s + 1) * tk].astype(BF16)
    r["kct"][...] = _dot_nt(r["wkct"][...], h) * (dkc ** -0.5)

    carry_r, carry_c = r["carry_r"], r["carry_c"]

    @pl.when(pl.program_id(1) == 0)
    def _():
        carry_r[...] = jnp.zeros_like(carry_r)
        carry_c[...] = jnp.zeros_like(carry_c)

    r2 = lax.broadcasted_iota(jnp.int32, (tm, tm), 0)
    c2 = lax.broadcasted_iota(jnp.int32, (tm, tm), 1)
    tril = jnp.where(c2 <= r2, 1.0, 0.0).astype(BF16)
    fcum = _dot_exact_l(tril, lsg) + carry_r[0:1, :]
    carry_r[0:1, :] = fcum[tm - 1:tm, :]

    zst = _dot_nt(r["wst"][...], h) + r["bsgt"][...]
    row = lax.broadcasted_iota(jnp.int32, zst.shape, 0)
    lsgt = jnp.where((row >= n_lf) & (row < n_lf + n_ic), zst, jax.nn.log_sigmoid(zst))
    r["sgt"][...] = lsgt
    triu = jnp.where(r2 <= c2, 1.0, 0.0).astype(BF16)
    ftc = _dot_exact_r(lsgt, triu) + carry_c[:, 0:1]
    r["ftc"][...] = ftc
    carry_c[:, 0:1] = ftc[:, tm - 1:tm]

    g_all = _dot_exact_r(fcum, r["sel"][...])
    lane = lax.broadcasted_iota(jnp.int32, (tm, LANES), 1)
    bl = BIAS_LANES_PER_HEAD
    is_one = (lane < bl // 2) | ((lane >= bl) & (lane < bl + bl // 2))
    piece = [(lane == bl // 2 + i) | (lane == bl + bl // 2 + i) for i in range(3)]
    for p in range(kn.shape[1] // LANES):
        hi, mid, lo = (v.astype(F32) for v in _split3(g_all[:, p * LANES:(p + 1) * LANES]))
        fsel = jnp.where(piece[0], hi, jnp.where(piece[1], mid, lo))
        blk = jnp.where(is_one, 1.0, jnp.where(lane < 2 * bl, -fsel, 0.0))
        r["kaug"][:, 2 * p * LANES:(2 * p + 1) * LANES] = kn[:, p * LANES:(p + 1) * LANES].astype(BF16)
        r["kaug"][:, (2 * p + 1) * LANES:(2 * p + 2) * LANES] = blk.astype(BF16)


def _proj(x, mod, pw, *, prompt, batch, tm, tk, chunk, groups, dh, n_lf, n_ic, stacked=None):
    m, d = x.shape
    s_len = m // batch
    nt = s_len // tm
    da = pw["wa"].shape[1] // 2
    db = pw["wk"].shape[1]
    dvc = pw["dvc"]
    dqk = pw["wc"].shape[1] - dvc
    nsg = pw["ws"].shape[1]

    def row(i, j):
        return (i * nt + j, 0)

    modspec, gspec = _mod_specs(mod, tm, lambda i, j: (i, i * nt + j))
    full = lambda a: pl.BlockSpec(a.shape, lambda i, j: (0,) * a.ndim, pipeline_mode=pl.Buffered(1))
    rowspec = lambda n: pl.BlockSpec((tm, n), row)
    colspec = lambda n: pl.BlockSpec((None, n, tm), lambda i, j: (i, 0, j))
    slabspec = lambda n: pl.BlockSpec((None, tm // tk, n, tk), lambda i, j: (i, j, 0, 0))
    layer, depth, prev = stacked if stacked is not None else (0, 1, None)
    layerspec = lambda n: pl.BlockSpec((None, None, n, tm), lambda i, j: (layer, i, 0, j))

    ins = [("x", x, rowspec(d)), ("mod", mod[0], modspec), ("g", mod[1], gspec)]
    wnames = (["wa", "wqt", "wk", "wkt", "wvt", "wc", "wkct", "ws", "wst", "gva", "gq_col", "gk_row", "gk_col",
               "bsg", "bsgt", "sel", "wmix", "bmix"] if prompt else
              ["wa", "wq", "wk", "wv", "wc", "wkc", "ws", "gva", "gq_row", "gk_row", "bsg", "wmix", "bmix"])
    ins += [(n, pw[n], full(pw[n])) for n in wnames]

    sds = jax.ShapeDtypeStruct
    if prompt:
        nsgt = pw["wst"].shape[0]
        outs = [("ya", sds((m, da), BF16), rowspec(da)),
                ("qt16", sds((batch, db, s_len), BF16), colspec(db)),
                ("kt32", sds((depth, batch, db, s_len), F32), layerspec(db)),
                ("kaug", sds((m, 2 * db), BF16), rowspec(2 * db)),
                ("vt32", sds((depth, batch, db, s_len), F32), layerspec(db)),
                ("vt16", sds((batch, s_len // tk, db, tk), BF16), slabspec(db)),
                ("qc", sds((m, dqk), F32), rowspec(dqk)),
                ("kct", sds((batch, dqk, s_len), F32), colspec(dqk)),
                ("vc", sds((m, dvc), F32), rowspec(dvc)),
                ("sg", sds((m, nsg), F32), rowspec(nsg)),
                ("sgt", sds((depth, batch, nsgt, s_len), F32), layerspec(nsgt)),
                ("ftc", sds((batch, nsgt, s_len), F32), colspec(nsgt))]
        scratch = [("carry_r", pltpu.VMEM((8, nsg), F32)), ("carry_c", pltpu.VMEM((nsgt, LANES), F32))]
    else:
        outs = [("ya", sds((m, da), BF16), rowspec(da)), ("q32", sds((m, db), F32), rowspec(db)),
                ("k32", sds((m, db), F32), rowspec(db)), ("v32", sds((m, db), F32), rowspec(db)),
                ("qc", sds((m, dqk), F32), rowspec(dqk)), ("kc", sds((m, dqk), F32), rowspec(dqk)),
                ("vc", sds((m, dvc), F32), rowspec(dvc)), ("sg", sds((m, nsg), F32), rowspec(nsg)),
                ("va", sds((m, da), F32), rowspec(da))]
        scratch = []
    aliases = {}
    if prompt and prev is not None:
        out_names = [n for n, _, _ in outs]
        for n in STACKED_OUTPUTS:
            aliases[len(ins)] = out_names.index(n)
            ins.append(("prev_" + n, prev[n], pl.BlockSpec(memory_space=pl.ANY)))
    names = tuple(n for n, _, _ in ins) + tuple(n for n, _, _ in outs) + tuple(n for n, _ in scratch)
    kern = functools.partial(_proj_kernel, names=names, prompt=prompt, tm=tm, tk=tk, chunk=chunk, groups=groups,
                             dh=dh, n_lf=n_lf, n_ic=n_ic)
    res = pl.pallas_call(
        kern, grid=(batch, nt), in_specs=[s for _, _, s in ins], out_specs=[s for _, _, s in outs],
        out_shape=[o for _, o, _ in outs], scratch_shapes=[s for _, s in scratch],
        compiler_params=_params(("arbitrary", "arbitrary")), input_output_aliases=aliases,
        name="proj_prompt" if prompt else "proj_sample",
    )(*[a for _, a, _ in ins])
    return dict(zip([n for n, _, _ in outs], res))


def _fox_prompt_kernel(qt_ref, fq_ref, ka_ref, vt_ref, o_ref, qa_scr, st_scr, m_scr, l_scr, acc_scr, *, tq, tk, dh):
    qi = pl.program_id(1)
    n_pairs = qt_ref.shape[0] // LANES
    assert tq == 2 * tk, "the key-block schedule below pairs blocks: two key blocks per query block"
    bl = BIAS_LANES_PER_HEAD
    rowi = lax.broadcasted_iota(jnp.int32, (LANES, tq), 0)
    krow = lax.broadcasted_iota(jnp.int32, (tk, tq), 0)
    qcol = lax.broadcasted_iota(jnp.int32, (tk, tq), 1)

    for p in range(n_pairs):
        qpair = qt_ref[p * LANES:(p + 1) * LANES, :].astype(F32)
        for e in range(2):
            hi, mid, lo = (v.astype(F32) for v in _split3(fq_ref[2 * p + e:2 * p + e + 1, :]))
            top = jnp.where((rowi >= e * dh) & (rowi < (e + 1) * dh), qpair, 0.0)
            bot = jnp.where(rowi == bl * e, hi, jnp.where(rowi == bl * e + 1, mid, jnp.where(
                rowi == bl * e + 2, lo, jnp.where((rowi >= bl * e + 3) & (rowi < bl * e + 6), 1.0, 0.0))))
            qa_scr[e, 0:LANES, :] = top.astype(BF16)
            qa_scr[e, LANES:2 * LANES, :] = bot.astype(BF16)
        m_scr[...] = jnp.full_like(m_scr, -jnp.inf)
        l_scr[...] = jnp.zeros_like(l_scr)
        acc_scr[...] = jnp.zeros_like(acc_scr)

        def scores(kj, slot):
            ks = ka_ref[pl.ds(pl.multiple_of(kj * tk, tk), tk), 2 * p * LANES:(2 * p + 2) * LANES]
            for e in range(2):
                st_scr[slot, e] = _dot(ks, qa_scr[e])

        def absorb(kj, slot, diag_off):
            for e in range(2):
                st = st_scr[slot, e]
                if diag_off is not None:
                    st = jnp.where(krow + diag_off * tk <= qcol, st, -jnp.inf)
                m_prev = m_scr[e]
                m_new = jnp.maximum(m_prev, jnp.max(st, axis=0, keepdims=True))
                alpha = jnp.exp(m_prev - m_new)
                pt = jnp.exp(st - m_new)
                l_scr[e] = alpha * l_scr[e] + jnp.sum(pt, axis=0, keepdims=True)
                vth = vt_ref[kj, (2 * p + e) * dh:(2 * p + e + 1) * dh, :]
                acc_scr[e] = alpha * acc_scr[e] + _dot(vth, pt.astype(BF16))
                m_scr[e] = m_new

        scores(0, 0)

        def body(jj, carry):
            scores(2 * jj + 1, 1)
            absorb(2 * jj, 0, None)
            scores(2 * jj + 2, 0)
            absorb(2 * jj + 1, 1, None)
            return carry

        lax.fori_loop(0, qi, body, 0)
        scores(2 * qi + 1, 1)
        absorb(2 * qi, 0, 0)
        absorb(2 * qi + 1, 1, 1)
        out_t = jnp.concatenate([acc_scr[e] / l_scr[e] for e in range(2)], axis=0)
        o_ref[:, p * LANES:(p + 1) * LANES] = out_t.T.astype(BF16)


def _fox_prompt(qt, ftc, kaug, vt, *, tq, dh):
    batch, db, s_len = qt.shape
    nk, tk = vt.shape[1], vt.shape[3]
    nq = s_len // tq
    kern = functools.partial(_fox_prompt_kernel, tq=tq, tk=tk, dh=dh)
    return pl.pallas_call(
        kern, grid=(batch, nq),
        in_specs=[pl.BlockSpec((None, db, tq), lambda b, i: (b, 0, i)),
                  pl.BlockSpec((None, ftc.shape[1], tq), lambda b, i: (b, 0, i)),
                  pl.BlockSpec((s_len, kaug.shape[1]), lambda b, i: (b, 0)),
                  pl.BlockSpec((None, nk, db, tk), lambda b, i: (b, 0, 0, 0))],
        out_specs=pl.BlockSpec((tq, db), lambda b, i: (b * nq + i, 0)),
        out_shape=jax.ShapeDtypeStruct((batch * s_len, db), BF16),
        scratch_shapes=[pltpu.VMEM((2, 2 * LANES, tq), BF16), pltpu.VMEM((2, 2, tk, tq), F32),
                        pltpu.VMEM((2, 1, tq), F32),
                        pltpu.VMEM((2, 1, tq), F32), pltpu.VMEM((2, dh, tq), F32)],
        compiler_params=_params(("arbitrary", "arbitrary")),
        name="fox_prompt",
    )(qt, ftc, kaug, vt)


def _fox_sample_kernel(pt_ref, q_ref, kn_ref, vn_ref, lfn_ref, *refs, n_tok, n_heads, dh, pages):
    del pt_ref
    k_refs = refs[:pages]
    v_refs = refs[pages:2 * pages]
    lf_refs = refs[2 * pages:3 * pages]
    o_ref, qs_scr, m_scr, l_scr, acc_scr, r_scr, cq_scr = refs[3 * pages:]
    j = pl.program_id(1)
    nrow = n_tok * n_heads
    dbw = q_ref.shape[-1]
    psz = kn_ref.shape[0]
    hrow = lax.broadcasted_iota(jnp.int32, (n_heads, dbw), 0)
    hcol = lax.broadcasted_iota(jnp.int32, (n_heads, dbw), 1)
    headmask = (hcol >= hrow * dh) & (hcol < (hrow + 1) * dh)
    r2 = lax.broadcasted_iota(jnp.int32, (psz, psz), 0)
    c2 = lax.broadcasted_iota(jnp.int32, (psz, psz), 1)

    def update(s, pv_fn):
        m_prev = m_scr[...]
        m_new = jnp.maximum(m_prev, jnp.max(s, axis=-1, keepdims=True))
        alpha = jnp.exp(m_prev - m_new)
        pm = jnp.exp(s - m_new)
        l_scr[...] = alpha * l_scr[...] + jnp.sum(pm, axis=-1, keepdims=True)
        acc_scr[...] = alpha * acc_scr[...] + pv_fn(pm.astype(BF16))
        m_scr[...] = m_new

    @pl.when(j == 0)
    def _():
        q = q_ref[...]
        qs = jnp.concatenate(
            [jnp.where(headmask, jnp.broadcast_to(q[t:t + 1, :], (n_heads, dbw)), 0.0)
             for t in range(n_tok)], axis=0).astype(BF16)
        qs_scr[...] = qs
        triu = jnp.where(r2 <= c2, 1.0, 0.0).astype(BF16)
        cn = _dot_exact_r(lfn_ref[...], triu)
        cq = jnp.concatenate([cn[:, t:t + 1] for t in range(n_tok)], axis=0)
        cq_scr[...] = cq
        bias = cq - jnp.concatenate([cn] * n_tok, axis=0)
        s = _dot_nt(qs, kn_ref[...]) + bias
        trow = lax.broadcasted_iota(jnp.int32, (nrow, psz), 0) // n_heads
        scol = lax.broadcasted_iota(jnp.int32, (nrow, psz), 1)
        s = jnp.where(scol <= trow, s, -jnp.inf)
        m_scr[...] = jnp.full_like(m_scr, -jnp.inf)
        l_scr[...] = jnp.zeros_like(l_scr)
        acc_scr[...] = jnp.zeros_like(acc_scr)
        r_scr[...] = jnp.zeros_like(r_scr)
        update(s, lambda pm: _dot(pm, vn_ref[...]))

    qs = qs_scr[...]
    cq = cq_scr[...]
    later = jnp.where(r2 > c2, 1.0, 0.0).astype(BF16)
    lf_all = jnp.concatenate([lf_refs[i][...] for i in range(pages)], axis=0)
    suf_all = _dot_exact_r(lf_all, later)
    tot_all = jnp.sum(lf_all, axis=-1, keepdims=True)
    rsum = r_scr[:, 0:1]
    parts = []
    for i in range(pages):
        suf = suf_all[i * n_heads:(i + 1) * n_heads, :] + rsum
        parts.append(_dot(qs, k_refs[i][...].astype(BF16)) + (cq + jnp.concatenate([suf] * n_tok, axis=0)))
        rsum = rsum + tot_all[i * n_heads:(i + 1) * n_heads, :]
    r_scr[:, 0:1] = rsum

    def pv_pages(pm):
        acc = None
        for i in range(pages):
            term = _dot_nt(pm[:, i * psz:(i + 1) * psz], v_refs[i][...].astype(BF16))
            acc = term if acc is None else acc + term
        return acc

    update(jnp.concatenate(parts, axis=1), pv_pages)

    @pl.when(j == pl.num_programs(1) - 1)
    def _():
        out = acc_scr[...] / l_scr[...]
        for t in range(n_tok):
            blk = jnp.where(headmask, out[t * n_heads:(t + 1) * n_heads, :], 0.0)
            o_ref[t:t + 1, :] = jnp.sum(blk, axis=0, keepdims=True).astype(o_ref.dtype)


def _fox_sample(page_table, q, k_new, v_new, lft_new, cache_kt, cache_vt, cache_lft, layer, *, n_heads, dh, pages):
    bd, n_tok, dbw = q.shape
    n_pages = page_table.shape[1]
    psz = cache_kt.shape[-1]
    steps = n_pages // pages
    nrow = n_tok * n_heads

    def page_map(i):
        return lambda b, j, pt: (layer, pt[b, n_pages - 1 - (j * pages + i)], 0, 0)

    per_b = lambda shape: pl.BlockSpec((None,) + shape, lambda b, j, pt: (b, 0, 0))
    in_specs = [per_b((n_tok, dbw)), per_b((psz, dbw)), per_b((psz, dbw)), per_b((n_heads, psz))]
    in_specs += [pl.BlockSpec((None, None, dbw, psz), page_map(i)) for i in range(pages)]
    in_specs += [pl.BlockSpec((None, None, dbw, psz), page_map(i)) for i in range(pages)]
    in_specs += [pl.BlockSpec((None, None, n_heads, psz), page_map(i)) for i in range(pages)]
    kern = functools.partial(_fox_sample_kernel, n_tok=n_tok, n_heads=n_heads, dh=dh, pages=pages)
    grid_spec = pltpu.PrefetchScalarGridSpec(
        num_scalar_prefetch=1, grid=(bd, steps), in_specs=in_specs,
        out_specs=pl.BlockSpec((None, n_tok, dbw), lambda b, j, pt: (b, 0, 0)),
        scratch_shapes=[pltpu.VMEM((nrow, dbw), BF16), pltpu.VMEM((nrow, 1), F32), pltpu.VMEM((nrow, 1), F32),
                        pltpu.VMEM((nrow, dbw), F32), pltpu.VMEM((n_heads, LANES), F32),
                        pltpu.VMEM((nrow, 1), F32)])
    return pl.pallas_call(
        kern, grid_spec=grid_spec, out_shape=jax.ShapeDtypeStruct((bd, n_tok, dbw), F32),
        compiler_params=_params(("arbitrary", "arbitrary")), name="fox_sample",
    )(page_table, q, k_new, v_new, lft_new, *([cache_kt] * pages), *([cache_vt] * pages), *([cache_lft] * pages))


def _mlstm_kernel(q_ref, kt_ref, v_ref, sg_ref, c0_ref, n0_ref, m0_ref, h_ref, st_ref, mo_ref, st_scr, m_scr,
                  *, n_heads, dk, dv, valid, ic_col, lf_col):
    c = pl.program_id(1)
    L = q_ref.shape[0]
    dqk = n_heads * dk

    @pl.when(c == 0)
    def _():
        st_scr[:, :dv] = c0_ref[...]
        st_scr[:, dv:] = jnp.broadcast_to(n0_ref[...], (dqk, st_scr.shape[1] - dv))
        for hh in range(n_heads):
            m_scr[hh] = jnp.broadcast_to(m0_ref[hh:hh + 1, 0:1], (L, L))

    q = q_ref[...]
    kt = _split2(kt_ref[...])
    sg = sg_ref[...]
    row = lax.broadcasted_iota(jnp.int32, (L, L), 0)
    col = lax.broadcasted_iota(jnp.int32, (L, L), 1)
    tril = jnp.where(col <= row, 1.0, 0.0).astype(BF16)
    ones = jnp.ones((L, L), BF16)
    eye = col == row
    mask = (col <= row) & (col < valid)
    qlane = lax.broadcasted_iota(jnp.int32, (L, dqk), 1)
    st_pair = _split2(st_scr[...])

    wide = lambda f: jnp.concatenate([f(hh) for hh in range(n_heads)], axis=1)
    lf_w = wide(lambda hh: jnp.broadcast_to(sg[:, lf_col + hh:lf_col + hh + 1], (L, L)))
    ig_w = wide(lambda hh: jnp.broadcast_to(sg[:, ic_col + hh:ic_col + hh + 1], (L, L)))
    b_w = _dot_exact_l(tril, lf_w)
    a_w = ig_w - b_w
    a_rw = _dot_exact_l(ones, jnp.where(wide(lambda hh: eye), ig_w, 0.0)
                        - jnp.where(wide(lambda hh: col >= row), lf_w, 0.0))
    qms = [_split2(jnp.where((qlane >= hh * dk) & (qlane < (hh + 1) * dk), q, 0.0)) for hh in range(n_heads)]
    qk_all = [_dot_hilo(qm, kt) for qm in qms]
    qcn_all = [_dot_hilo(qm, st_pair) for qm in qms]

    for hh in range(n_heads):
        kth = tuple(part[hh * dk:(hh + 1) * dk, :] for part in kt)
        vh = v_ref[:, hh * dv:(hh + 1) * dv]
        b_c = b_w[:, hh * L:(hh + 1) * L]
        a_c = a_w[:, hh * L:(hh + 1) * L]
        a_r = a_rw[:, hh * L:(hh + 1) * L]
        qcn = qcn_all[hh]
        m_old = m_scr[hh]
        dmat = jnp.where(mask, b_c + a_r, -jnp.inf)
        inter = b_c + m_old
        m_t = jnp.maximum(jnp.max(dmat, axis=-1, keepdims=True), inter)
        w_intra = jnp.exp(dmat - m_t) * qk_all[hh]
        w_inter = jnp.exp(inter - m_t)
        num = _dot_hilo(_split2(w_intra), _split2(vh)) + w_inter[:, 0:1] * qcn[:, :dv]
        den = jnp.sum(w_intra, axis=-1, keepdims=True) + w_inter[:, 0:1] * qcn[:, dv:dv + 1]
        hout = num / jnp.maximum(jnp.abs(den), jnp.exp(-m_t[:, 0:1]))
        h_ref[:, hh * dv:(hh + 1) * dv] = hout

        m_new = jnp.broadcast_to(m_t[valid - 1:valid, :], (L, L))
        b_last = jnp.broadcast_to(b_c[valid - 1:valid, :], (L, L))
        w_k = jnp.where(row < valid, jnp.exp(b_last + a_c - m_new), 0.0)
        decay = jnp.exp(b_last + m_old - m_new)[0:1, 0:1]
        rhs = jnp.concatenate([w_k[:, 0:1] * vh, jnp.broadcast_to(w_k[:, 0:1], (L, st_scr.shape[1] - dv))],
                              axis=1)
        upd = _dot_hilo(kth, _split2(rhs))
        st_scr[hh * dk:(hh + 1) * dk, :] = decay * st_scr[hh * dk:(hh + 1) * dk, :] + upd
        m_scr[hh] = m_new

    @pl.when(c == pl.num_programs(1) - 1)
    def _():
        st_ref[...] = st_scr[...]
        for hh in range(n_heads):
            mo_ref[hh:hh + 1, :] = m_scr[hh][0:1, :]


def _mlstm(q, kt, v, sg, c0t, n0, m0, *, n_heads, dk, dv, valid, ic_col, lf_col):
    batch, s_len, dqk = q.shape
    L = MLSTM_CHUNK
    nc = s_len // L
    wst = 2 * dv
    kern = functools.partial(_mlstm_kernel, n_heads=n_heads, dk=dk, dv=dv, valid=valid, ic_col=ic_col, lf_col=lf_col)
    return pl.pallas_call(
        kern, grid=(batch, nc),
        in_specs=[pl.BlockSpec((None, L, dqk), lambda b, c: (b, c, 0)),
                  pl.BlockSpec((None, dqk, L), lambda b, c: (b, 0, c)),
                  pl.BlockSpec((None, L, n_heads * dv), lambda b, c: (b, c, 0)),
                  pl.BlockSpec((None, L, sg.shape[2]), lambda b, c: (b, c, 0)),
                  pl.BlockSpec((None, dqk, dv), lambda b, c: (b, 0, 0)),
                  pl.BlockSpec((None, dqk, 1), lambda b, c: (b, 0, 0)),
                  pl.BlockSpec((None, n_heads, 1), lambda b, c: (b, 0, 0))],
        out_specs=[pl.BlockSpec((None, L, n_heads * dv), lambda b, c: (b, c, 0)),
                   pl.BlockSpec((None, dqk, wst), lambda b, c: (b, 0, 0)),
                   pl.BlockSpec((None, n_heads, L), lambda b, c: (b, 0, 0))],
        out_shape=[jax.ShapeDtypeStruct((batch, s_len, n_heads * dv), F32),
                   jax.ShapeDtypeStruct((batch, dqk, wst), F32),
                   jax.ShapeDtypeStruct((batch, n_heads, L), F32)],
        scratch_shapes=[pltpu.VMEM((dqk, wst), F32), pltpu.VMEM((n_heads, L, L), F32)],
        compiler_params=_params(("arbitrary", "arbitrary")),
        name="mlstm",
    )(q, kt, v, sg, c0t, n0, m0)


def _merge_kernel(x_ref, mod_ref, g_ref, ya_ref, yb_ref, hc_ref,
                  woc_ref, wgt_ref, wbr_ref, wout_ref, o_ref):
    x = x_ref[...]
    h = _modnorm(x, g_ref[...], mod_ref[0], mod_ref[1]).astype(BF16)
    oc = jax.nn.sigmoid(_dot(h, woc_ref[...]))
    ys = (ya_ref[...], yb_ref[...], (oc * hc_ref[...]).astype(BF16))
    d = x.shape[1]
    acc = None
    for nb in range(len(ys)):
        gate = jax.nn.sigmoid(_dot(h, wgt_ref[:, nb * d:(nb + 1) * d]))
        term = gate * _dot(ys[nb], wbr_ref[nb])
        acc = term if acc is None else acc + term
    o_ref[...] = x + mod_ref[2] * _dot(acc.astype(BF16), wout_ref[...])


def _merge(x, mod, ya, yb, hc, mw, *, tm, tiles_per_group):
    m, d = x.shape
    modspec, gspec = _mod_specs(mod, tm, lambda i: (i // tiles_per_group, i))
    full = lambda a: pl.BlockSpec(a.shape, lambda i: (0,) * a.ndim, pipeline_mode=pl.Buffered(1))
    rowspec = lambda n: pl.BlockSpec((tm, n), lambda i: (i, 0))
    ws = [mw["woc"], mw["wgt"], mw["wbr"], mw["wout"]]
    return pl.pallas_call(
        _merge_kernel, grid=(m // tm,),
        in_specs=[rowspec(d), modspec, gspec, rowspec(ya.shape[1]), rowspec(yb.shape[1]),
                  rowspec(hc.shape[1])] + [full(w) for w in ws],
        out_specs=rowspec(d),
        out_shape=jax.ShapeDtypeStruct((m, d), F32),
        compiler_params=_params(("arbitrary",)),
        name="merge",
    )(x, mod[0], mod[1], ya, yb, hc, *ws)


def _pick_tile(n, pref):
    t = min(n, pref)
    while n % t:
        t //= 2
    return t


def kernel(x_prompt, x_sample, cache_k, cache_v, cache_logf, state_C, state_n, state_m, page_table,
           c_prompt, c_sample, w_ada, b_ada, g_norm, w_ff_up, w_ff_down, w_in, g_va, w_s, b_s,
           g_qb, g_kb, b_fb, b_ic, b_fc, w_branch, w_out):
    B, S, D = x_prompt.shape
    Bd, T, _ = x_sample.shape
    depth = w_ada.shape[0]
    n_pool, psz, HB, dh = cache_k.shape[1:]
    HC, DV, DK = state_C.shape[2:]
    GA, chunk_a = w_s.shape[1], w_s.shape[2]
    DA = g_va.shape[1]
    DB = HB * dh
    DQK = HC * DK
    DC = HC * DV
    NBR, BW = w_branch.shape[1], w_branch.shape[2]
    n_sub = g_norm.shape[1]
    Ms = Bd * T

    splits = (DA, DA, DB, DB, DB, HB, DQK, DQK, DC, HC, HC, DC, NBR * D)
    offs = [0]
    for sz in splits:
        offs.append(offs[-1] + sz)
    (o_ua, o_va, o_qb, o_kb, o_vb, o_fb, o_qc, o_kc, o_vc, o_ic, o_fc, o_oc, o_gt, o_end) = offs

    w_in_t = jnp.swapaxes(w_in, 1, 2)
    nsg = LANES
    nsgt = 2 * ((HB + 2 * HC + 15) // 16) * 8
    n_small = HB + 2 * HC

    def cols(l, a, b):
        return jnp.swapaxes(w_in_t[l, a:b], 0, 1).astype(BF16)

    def rows(l, a, b):
        return w_in_t[l, a:b].astype(BF16)

    w_up16 = w_ff_up.astype(BF16)
    w_down16 = w_ff_down.astype(BF16)
    w_br16 = w_branch.astype(BF16)
    w_out16 = w_out.astype(BF16)

    sel_np = np.zeros((nsg, (HB // 2) * LANES), np.float32)
    for hd in range(HB):
        base = (hd // 2) * LANES + (hd % 2) * BIAS_LANES_PER_HEAD + BIAS_LANES_PER_HEAD // 2
        sel_np[hd, base:base + 3] = 1.0
    sel = jnp.asarray(sel_np, BF16)

    layer_w = []
    for l in range(depth):
        small_rows = jnp.concatenate([w_in_t[l, o_fb:o_qc], w_in_t[l, o_ic:o_oc]], axis=0)
        wst = jnp.zeros((nsgt, D), F32).at[:n_small].set(small_rows).astype(BF16)
        ws = jnp.zeros((D, nsg), F32).at[:, :n_small].set(small_rows.T).astype(BF16)
        bias_small = jnp.concatenate([b_fb[l], b_ic[l], b_fc[l]])
        bsg = jnp.zeros((1, nsg), F32).at[0, :n_small].set(bias_small)
        bsgt = jnp.zeros((nsgt, 1), F32).at[:n_small, 0].set(bias_small)
        common = dict(
            wa=cols(l, o_ua, o_qb), wk=cols(l, o_kb, o_vb),
            wc=jnp.concatenate([cols(l, o_qc, o_kc), cols(l, o_vc, o_ic)], axis=1), dvc=DC,
            ws=ws, gva=g_va[l].reshape(1, DA), gk_row=jnp.tile(g_kb[l], HB).reshape(1, DB), bsg=bsg)
        wblk = jnp.einsum("ab,gts->gatbs", jnp.eye(Bd, dtype=F32), w_s[l][:, :T, :T]).reshape(GA, Ms, Ms)
        pw_p = dict(common, wqt=rows(l, o_qb, o_kb), wkt=rows(l, o_kb, o_vb), wvt=rows(l, o_vb, o_fb),
                    wkct=rows(l, o_kc, o_vc), wst=wst, gq_col=g_qb[l].reshape(dh, 1), gk_col=g_kb[l].reshape(dh, 1),
                    bsgt=bsgt, sel=sel, wmix=w_s[l], bmix=b_s[l].T)
        pw_s = dict(common, wq=cols(l, o_qb, o_kb), wv=cols(l, o_vb, o_fb), wkc=cols(l, o_kc, o_vc),
                    gq_row=jnp.tile(g_qb[l], HB).reshape(1, DB), wmix=wblk,
                    bmix=jnp.tile(b_s[l][:, :T], (1, Bd)).T)
        mw = dict(woc=cols(l, o_oc, o_gt), wgt=cols(l, o_gt, o_end), wbr=w_br16[l], wout=w_out16[l])
        layer_w.append((pw_p, pw_s, mw))

    r_all = B + Bd
    r_pad = -(-r_all // 8) * 8
    c_all = jnp.zeros((r_pad, D), F32).at[:B].set(c_prompt).at[B:r_all].set(c_sample)
    mods = _ada_mod(c_all, w_ada, b_ada).reshape(depth, r_pad, n_sub, 3, D)
    mods_p = jnp.transpose(mods[:, :B], (0, 2, 1, 3, 4)).reshape(depth, n_sub, B, 3, 1, D)
    mods_s = jnp.transpose(jnp.repeat(mods[:, B:r_all], T, axis=1), (0, 2, 3, 1, 4))
    gains = g_norm.reshape(depth, n_sub, 1, D)

    cache_kt = jnp.transpose(cache_k, (0, 1, 3, 4, 2)).reshape(depth, n_pool, DB, psz)
    cache_vt = jnp.transpose(cache_v, (0, 1, 3, 4, 2)).reshape(depth, n_pool, DB, psz)
    cache_lft = jnp.transpose(cache_logf, (0, 1, 3, 2))

    tm_p = _pick_tile(S, 512)
    tm_proj = _pick_tile(S, 512)
    tk = _pick_tile(S, 256)
    tq = 2 * tk
    pages = _pick_tile(page_table.shape[1], 32)

    xp = x_prompt.reshape(B * S, D)
    xs = x_sample.reshape(Ms, D)
    outs_p, outs_s, chunk_v = [], [], []
    pr = None
    for l in range(depth):
        pw_p, pw_s, mw = layer_w[l]

        mod_p = lambda sub: (mods_p, gains, l, sub)
        mod_s = lambda sub: (mods_s, gains, l, sub)

        xp = _ffn(xp, mod_p(0), w_up16, w_down16, l, 0, tm_p, S // tm_p)
        pr = _proj(xp, mod_p(1), pw_p, prompt=True, batch=B, tm=tm_proj, tk=tk, chunk=chunk_a, groups=GA,
                   dh=dh, n_lf=HB, n_ic=HC, stacked=(l, depth, pr))
        yb =_fox_prompt(pr["qt16"], pr["ftc"], pr["kaug"], pr["vt16"], tq=tq, dh=dh)
        hc, st, mo = _mlstm(pr["qc"].reshape(B, S, DQK), pr["kct"], pr["vc"].reshape(B, S, DC),
                            pr["sg"].reshape(B, S, nsg),
                            jnp.zeros((B, DQK, DV), F32), jnp.zeros((B, DQK, 1), F32), jnp.zeros((B, HC, 1), F32),
                            n_heads=HC, dk=DK, dv=DV, valid=MLSTM_CHUNK, ic_col=HB, lf_col=HB + HC)
        xp = _merge(xp, mod_p(1), pr["ya"], yb, hc.reshape(B * S, DC), mw, tm=tm_proj,
                    tiles_per_group=S // tm_proj)
        xp = _ffn(xp, mod_p(2), w_up16, w_down16, l, 1, tm_p, S // tm_p)
        outs_p.append((
            jnp.swapaxes(st[:, :, :DV].reshape(B, HC, DK, DV), -1, -2),
            st[:, :, DV].reshape(B, HC, DK),
            mo[:, :, 0]))

        xs = _ffn(xs, mod_s(0), w_up16, w_down16, l, 0, Ms, 1)
        ps = _proj(xs, mod_s(1), pw_s, prompt=False, batch=1, tm=Ms, tk=Ms, chunk=Ms, groups=GA, dh=dh,
                   n_lf=HB, n_ic=HC)
        k32_s, v32_s, sg_s = ps["k32"], ps["v32"], ps["sg"]
        pad_t = lambda a: jnp.pad(a.reshape(Bd, T, -1), ((0, 0), (0, psz - T), (0, 0)))
        lft_new = jnp.swapaxes(pad_t(sg_s[:, :HB]), 1, 2)
        yb_s = _fox_sample(page_table, ps["q32"].reshape(Bd, T, DB), pad_t(k32_s).astype(BF16),
                           pad_t(v32_s).astype(BF16), lft_new, cache_kt, cache_vt, cache_lft, l,
                           n_heads=HB, dh=dh, pages=pages)
        padc = lambda a: jnp.pad(a.reshape(Bd, T, -1), ((0, 0), (0, MLSTM_CHUNK - T), (0, 0)))
        hc_s, st_s, mo_s = _mlstm(
            padc(ps["qc"]), jnp.swapaxes(padc(ps["kc"]), 1, 2), padc(ps["vc"]), padc(sg_s),
            jnp.swapaxes(state_C[l], -1, -2).reshape(Bd, DQK, DV), state_n[l].reshape(Bd, DQK, 1),
            state_m[l].reshape(Bd, HC, 1), n_heads=HC, dk=DK, dv=DV, valid=T, ic_col=HB, lf_col=HB + HC)
        xs = _merge(xs, mod_s(1), ps["ya"], yb_s.reshape(Ms, DB).astype(BF16), hc_s[:, :T].reshape(Ms, DC),
                    mw, tm=Ms, tiles_per_group=1)
        xs = _ffn(xs, mod_s(2), w_up16, w_down16, l, 1, Ms, 1)
        outs_s.append((
            k32_s.reshape(Bd, T, HB, dh), v32_s.reshape(Bd, T, HB, dh), sg_s[:, :HB].reshape(Bd, T, HB),
            jnp.swapaxes(st_s[:, :, :DV].reshape(Bd, HC, DK, DV), -1, -2),
            st_s[:, :, DV].reshape(Bd, HC, DK),
            mo_s[:, :, 0]))
        chunk_v.append(ps["va"].reshape(Bd, T, DA))

    stk = lambda states, i: jnp.stack([s[i] for s in states])
    return (xp.reshape(B, S, D), xs.reshape(Bd, T, D),
            jnp.transpose(pr["kt32"].reshape(depth, B, HB, dh, S), (0, 1, 4, 2, 3)),
            jnp.transpose(pr["vt32"].reshape(depth, B, HB, dh, S), (0, 1, 4, 2, 3)),
            jnp.transpose(pr["sgt"][:, :, :HB, :], (0, 1, 3, 2)),
            stk(outs_p, 0), stk(outs_p, 1), stk(outs_p, 2),
            stk(outs_s, 0), stk(outs_s, 1), stk(outs_s, 2), stk(outs_s, 3), stk(outs_s, 4), stk(outs_s, 5),
            jnp.stack(chunk_v))
```

```python
import functools

import numpy as np
import jax
import jax.numpy as jnp
from jax import lax
from jax.experimental import pallas as pl
from jax.experimental.pallas import tpu as pltpu

F32 = jnp.float32
BF16 = jnp.bfloat16
EPS = 1e-6
FFN_RES = 0.5
LANES = 128
LOG2E = 1.4426950408889634
MLSTM_CHUNK = 128
VMEM_LIMIT = 56 * 1024 * 1024
NT_DIMS = (((1,), (1,)), ((), ()))


def _dot(a, b):
    return jnp.dot(a, b, preferred_element_type=F32)


def _dot_nt(a, b):
    return lax.dot_general(a, b, NT_DIMS, preferred_element_type=F32)


def _split3(x):
    hi = x.astype(BF16)
    r = x - hi.astype(F32)
    mid = r.astype(BF16)
    lo = (r - mid.astype(F32)).astype(BF16)
    return hi, mid, lo


def _split2(x):
    hi = x.astype(BF16)
    return hi, (x - hi.astype(F32)).astype(BF16)


def _dot_hilo(a, b):
    return _dot(a[0], b[0]) + (_dot(a[0], b[1]) + _dot(a[1], b[0]))


def _dot_exact_l(a, x):
    hi, mid, lo = _split3(x)
    return _dot(a, hi) + _dot(a, mid) + _dot(a, lo)


def _dot_exact_r(x, a):
    hi, mid, lo = _split3(x)
    return _dot(hi, a) + _dot(mid, a) + _dot(lo, a)


def _modnorm(x, g, shift, scale):
    y = x * lax.rsqrt(jnp.mean(x * x, axis=-1, keepdims=True) + EPS)
    return (y * g) * (1 + scale) + shift


def _params(sem):
    return pltpu.CompilerParams(dimension_semantics=sem, vmem_limit_bytes=VMEM_LIMIT)


def _ada_kernel(c_ref, w_ref, b_ref, o_ref):
    s = jax.nn.silu(c_ref[...]).astype(BF16)
    o_ref[...] = _dot(s, w_ref[...].astype(BF16)) + b_ref[...]


def _ada_mod(c_all, w_ada, b_ada):
    depth, d, n = w_ada.shape
    r = c_all.shape[0]
    tn = n // 8
    return pl.pallas_call(
        _ada_kernel,
        grid=(depth, n // tn),
        in_specs=[pl.BlockSpec((r, d), lambda l, j: (0, 0)),
                  pl.BlockSpec((None, d, tn), lambda l, j: (l, 0, j)),
                  pl.BlockSpec((None, 1, tn), lambda l, j: (l, 0, j))],
        out_specs=pl.BlockSpec((None, r, tn), lambda l, j: (l, 0, j)),
        out_shape=jax.ShapeDtypeStruct((depth, r, n), F32),
        compiler_params=_params(("arbitrary", "arbitrary")),
        name="ada_mod",
    )(c_all, w_ada, b_ada.reshape(depth, 1, n))


def _mod_specs(mod, tm, tile_index):
    arr, gain, l, sub = mod
    d = arr.shape[-1]
    if arr.ndim == 6:
        mspec = pl.BlockSpec((None, None, None, 3, 1, d), lambda *g: (l, sub, tile_index(*g)[0], 0, 0, 0))
    else:
        mspec = pl.BlockSpec((None, None, 3, tm, d), lambda *g: (l, sub, 0, tile_index(*g)[1], 0))
    return mspec, pl.BlockSpec((None, None, 1, d), lambda *g: (l, sub, 0, 0))


def _ffn_kernel(x_ref, mod_ref, g_ref, wu_ref, wd_ref, o_ref, *, n_chunks):
    x = x_ref[...]
    h = _modnorm(x, g_ref[...], mod_ref[0], mod_ref[1]).astype(BF16)
    dff = wd_ref.shape[0]
    tf = dff // n_chunks
    acc = None
    for j in range(n_chunks):
        a = _dot(h, wu_ref[:, j * tf:(j + 1) * tf])
        b = _dot(h, wu_ref[:, dff + j * tf:dff + (j + 1) * tf])
        act = (jax.nn.silu(a) * b).astype(BF16)
        term = _dot(act, wd_ref[j * tf:(j + 1) * tf, :])
        acc = term if acc is None else acc + term
    o_ref[...] = x + (FFN_RES * mod_ref[2]) * acc


def _ffn(x, mod, w_up, w_down, l, k, tm, tiles_per_group):
    m, d = x.shape
    dff = w_down.shape[2]
    n_chunks = 2 if (dff // 2) % LANES == 0 else 1
    modspec, gspec = _mod_specs(mod, tm, lambda i: (i // tiles_per_group, i))
    resident = lambda shape: pl.BlockSpec((None, None) + shape, lambda i: (l, k, 0, 0), pipeline_mode=pl.Buffered(1))
    return pl.pallas_call(
        functools.partial(_ffn_kernel, n_chunks=n_chunks),
        grid=(m // tm,),
        in_specs=[pl.BlockSpec((tm, d), lambda i: (i, 0)), modspec, gspec,
                  resident((d, 2 * dff)), resident((dff, d))],
        out_specs=pl.BlockSpec((tm, d), lambda i: (i, 0)),
        out_shape=jax.ShapeDtypeStruct((m, d), F32),
        compiler_params=_params(("arbitrary",)),
        name="ffn",
    )(x, mod[0], mod[1], w_up, w_down)


def _group_rms_lanes(z, gsz):
    n = z.shape[-1]
    zz = z * z
    parts = []
    lane = lax.broadcasted_iota(jnp.int32, (z.shape[0], LANES), 1)
    for p in range(n // LANES):
        blk = zz[:, p * LANES:(p + 1) * LANES]
        scale = jnp.zeros_like(blk)
        for r in range(LANES // gsz):
            sel = (lane >= r * gsz) & (lane < (r + 1) * gsz)
            ms = jnp.sum(jnp.where(sel, blk, 0.0), axis=-1, keepdims=True) / gsz
            scale = jnp.where(sel, lax.rsqrt(ms + EPS), scale)
        parts.append(scale)
    return jnp.concatenate(parts, axis=-1)


def _rms_rows(zt, dh, gcol):
    outs = []
    for hh in range(zt.shape[0] // dh):
        blk = zt[hh * dh:(hh + 1) * dh, :]
        ms = jnp.mean(blk * blk, axis=0, keepdims=True)
        outs.append((blk * lax.rsqrt(ms + EPS)) * gcol)
    return outs


BIAS_LANES_PER_HEAD = 6
STACKED_OUTPUTS = ("kt32", "vt32", "sgt")


def _proj_kernel(*refs, names, prompt, tm, tk, chunk, groups, dh, n_lf, n_ic):
    r = dict(zip(names, refs))
    h = _modnorm(r["x"][...], r["g"][...], r["mod"][0], r["mod"][1]).astype(BF16)

    za = _dot(h, r["wa"][...])
    da = za.shape[1] // 2
    dg = da // groups
    ua = jax.nn.gelu(za[:, :da])
    vg = jax.nn.gelu(za[:, da:])
    va = (vg * lax.rsqrt(jnp.mean(vg * vg, axis=-1, keepdims=True) + EPS)) * r["gva"][...]
    if "va" in r:
        r["va"][...] = va
    vab = va.astype(BF16)
    rr = lax.broadcasted_iota(jnp.int32, (chunk, chunk), 0)
    cc = lax.broadcasted_iota(jnp.int32, (chunk, chunk), 1)
    for gi in range(groups):
        w = jnp.where(cc <= rr, r["wmix"][gi], 0.0).astype(BF16)
        bcol = r["bmix"][:, gi:gi + 1]
        for c in range(tm // chunk):
            rs = slice(c * chunk, (c + 1) * chunk)
            cs = slice(gi * dg, (gi + 1) * dg)
            mix = _dot(w, vab[rs, cs]) + bcol
            r["ya"][rs, cs] = (ua[rs, cs] * mix).astype(BF16)

    zs = _dot(h, r["ws"][...]) + r["bsg"][...]
    col = lax.broadcasted_iota(jnp.int32, zs.shape, 1)
    lsg = jnp.where((col >= n_lf) & (col < n_lf + n_ic), zs, jax.nn.log_sigmoid(zs))
    r["sg"][...] = lsg

    zc = _dot(h, r["wc"][...])
    dqk = r["qc"].shape[-1]
    dkc = dqk // n_ic
    r["qc"][...] = zc[:, :dqk]
    r["vc"][...] = zc[:, dqk:]

    scale = dh ** -0.5
    zk = _dot(h, r["wk"][...])
    kn = (zk * _group_rms_lanes(zk, dh)) * r["gk_row"][...]
    if not prompt:
        zq = _dot(h, r["wq"][...])
        r["q32"][...] = ((zq * _group_rms_lanes(zq, dh)) * r["gq_row"][...]) * scale
        r["k32"][...] = kn
        r["v32"][...] = _dot(h, r["wv"][...])
        r["kc"][...] = _dot(h, r["wkc"][...]) * (dkc ** -0.5)
        return

    for hh, blk in enumerate(_rms_rows(_dot_nt(r["wqt"][...], h), dh, r["gq_col"][...])):
        r["qt16"][hh * dh:(hh + 1) * dh, :] = (blk * (scale * LOG2E)).astype(BF16)
    for hh, blk in enumerate(_rms_rows(_dot_nt(r["wkt"][...], h), dh, r["gk_col"][...])):
        r["kt32"][hh * dh:(hh + 1) * dh, :] = blk
    vt = _dot_nt(r["wvt"][...], h)
    r["vt32"][...] = vt
    for s in range(tm // tk):
        r["vt16"][s] = vt[:, s * tk:(s + 1) * tk].astype(BF16)
    r["kct"][...] = _dot_nt(r["wkct"][...], h) * (dkc ** -0.5)

    carry_r, carry_c = r["carry_r"], r["carry_c"]

    @pl.when(pl.program_id(1) == 0)
    def _():
        carry_r[...] = jnp.zeros_like(carry_r)
        carry_c[...] = jnp.zeros_like(carry_c)

    r2 = lax.broadcasted_iota(jnp.int32, (tm, tm), 0)
    c2 = lax.broadcasted_iota(jnp.int32, (tm, tm), 1)
    tril = jnp.where(c2 <= r2, 1.0, 0.0).astype(BF16)
    fcum = _dot_exact_l(tril, lsg) + carry_r[0:1, :]
    carry_r[0:1, :] = fcum[tm - 1:tm, :]

    zst = _dot_nt(r["wst"][...], h) + r["bsgt"][...]
    row = lax.broadcasted_iota(jnp.int32, zst.shape, 0)
    lsgt = jnp.where((row >= n_lf) & (row < n_lf + n_ic), zst, jax.nn.log_sigmoid(zst))
    r["sgt"][...] = lsgt
    triu = jnp.where(r2 <= c2, 1.0, 0.0).astype(BF16)
    ftc = _dot_exact_r(lsgt, triu) + carry_c[:, 0:1]
    r["ftc"][...] = ftc
    carry_c[:, 0:1] = ftc[:, tm - 1:tm]

    g_all = _dot_exact_r(fcum, r["sel"][...]) * LOG2E
    lane = lax.broadcasted_iota(jnp.int32, (tm, LANES), 1)
    bl = BIAS_LANES_PER_HEAD
    is_one = (lane < bl // 2) | ((lane >= bl) & (lane < bl + bl // 2))
    piece = [(lane == bl // 2 + i) | (lane == bl + bl // 2 + i) for i in range(3)]
    for p in range(kn.shape[1] // LANES):
        hi, mid, lo = (v.astype(F32) for v in _split3(g_all[:, p * LANES:(p + 1) * LANES]))
        fsel = jnp.where(piece[0], hi, jnp.where(piece[1], mid, lo))
        blk = jnp.where(is_one, 1.0, jnp.where(lane < 2 * bl, -fsel, 0.0))
        r["kaug"][:, 2 * p * LANES:(2 * p + 1) * LANES] = kn[:, p * LANES:(p + 1) * LANES].astype(BF16)
        r["kaug"][:, (2 * p + 1) * LANES:(2 * p + 2) * LANES] = blk.astype(BF16)


def _proj(x, mod, pw, *, prompt, batch, tm, tk, chunk, groups, dh, n_lf, n_ic, stacked=None):
    m, d = x.shape
    s_len = m // batch
    nt = s_len // tm
    da = pw["wa"].shape[1] // 2
    db = pw["wk"].shape[1]
    dvc = pw["dvc"]
    dqk = pw["wc"].shape[1] - dvc
    nsg = pw["ws"].shape[1]

    def row(i, j):
        return (i * nt + j, 0)

    modspec, gspec = _mod_specs(mod, tm, lambda i, j: (i, i * nt + j))
    full = lambda a: pl.BlockSpec(a.shape, lambda i, j: (0,) * a.ndim, pipeline_mode=pl.Buffered(1))
    rowspec = lambda n: pl.BlockSpec((tm, n), row)
    colspec = lambda n: pl.BlockSpec((None, n, tm), lambda i, j: (i, 0, j))
    slabspec = lambda n: pl.BlockSpec((None, tm // tk, n, tk), lambda i, j: (i, j, 0, 0))
    layer, depth, prev = stacked if stacked is not None else (0, 1, None)
    layerspec = lambda n: pl.BlockSpec((None, None, n, tm), lambda i, j: (layer, i, 0, j))

    ins = [("x", x, rowspec(d)), ("mod", mod[0], modspec), ("g", mod[1], gspec)]
    wnames = (["wa", "wqt", "wk", "wkt", "wvt", "wc", "wkct", "ws", "wst", "gva", "gq_col", "gk_row", "gk_col",
               "bsg", "bsgt", "sel", "wmix", "bmix"] if prompt else
              ["wa", "wq", "wk", "wv", "wc", "wkc", "ws", "gva", "gq_row", "gk_row", "bsg", "wmix", "bmix"])
    ins += [(n, pw[n], full(pw[n])) for n in wnames]

    sds = jax.ShapeDtypeStruct
    if prompt:
        nsgt = pw["wst"].shape[0]
        outs = [("ya", sds((m, da), BF16), rowspec(da)),
                ("qt16", sds((batch, db, s_len), BF16), colspec(db)),
                ("kt32", sds((depth, batch, db, s_len), F32), layerspec(db)),
                ("kaug", sds((m, 2 * db), BF16), rowspec(2 * db)),
                ("vt32", sds((depth, batch, db, s_len), F32), layerspec(db)),
                ("vt16", sds((batch, s_len // tk, db, tk), BF16), slabspec(db)),
                ("qc", sds((m, dqk), F32), rowspec(dqk)),
                ("kct", sds((batch, dqk, s_len), F32), colspec(dqk)),
                ("vc", sds((m, dvc), F32), rowspec(dvc)),
                ("sg", sds((m, nsg), F32), rowspec(nsg)),
                ("sgt", sds((depth, batch, nsgt, s_len), F32), layerspec(nsgt)),
                ("ftc", sds((batch, nsgt, s_len), F32), colspec(nsgt))]
        scratch = [("carry_r", pltpu.VMEM((8, nsg), F32)), ("carry_c", pltpu.VMEM((nsgt, LANES), F32))]
    else:
        outs = [("ya", sds((m, da), BF16), rowspec(da)), ("q32", sds((m, db), F32), rowspec(db)),
                ("k32", sds((m, db), F32), rowspec(db)), ("v32", sds((m, db), F32), rowspec(db)),
                ("qc", sds((m, dqk), F32), rowspec(dqk)), ("kc", sds((m, dqk), F32), rowspec(dqk)),
                ("vc", sds((m, dvc), F32), rowspec(dvc)), ("sg", sds((m, nsg), F32), rowspec(nsg)),
                ("va", sds((m, da), F32), rowspec(da))]
        scratch = []
    aliases = {}
    if prompt and prev is not None:
        out_names = [n for n, _, _ in outs]
        for n in STACKED_OUTPUTS:
            aliases[len(ins)] = out_names.index(n)
            ins.append(("prev_" + n, prev[n], pl.BlockSpec(memory_space=pl.ANY)))
    names = tuple(n for n, _, _ in ins) + tuple(n for n, _, _ in outs) + tuple(n for n, _ in scratch)
    kern = functools.partial(_proj_kernel, names=names, prompt=prompt, tm=tm, tk=tk, chunk=chunk, groups=groups,
                             dh=dh, n_lf=n_lf, n_ic=n_ic)
    res = pl.pallas_call(
        kern, grid=(batch, nt), in_specs=[s for _, _, s in ins], out_specs=[s for _, _, s in outs],
        out_shape=[o for _, o, _ in outs], scratch_shapes=[s for _, s in scratch],
        compiler_params=_params(("arbitrary", "arbitrary")), input_output_aliases=aliases,
        name="proj_prompt" if prompt else "proj_sample",
    )(*[a for _, a, _ in ins])
    return dict(zip([n for n, _, _ in outs], res))


def _fox_prompt_kernel(qt_ref, fq_ref, ka_ref, vt_ref, o_ref, qa_scr, st_scr, m_scr, l_scr, acc_scr, *, tq, tk, dh):
    qi = pl.program_id(1)
    n_pairs = qt_ref.shape[0] // LANES
    assert tq == 2 * tk, "the key-block schedule below pairs blocks: two key blocks per query block"
    bl = BIAS_LANES_PER_HEAD
    rowi = lax.broadcasted_iota(jnp.int32, (LANES, tq), 0)
    krow = lax.broadcasted_iota(jnp.int32, (tk, tq), 0)
    qcol = lax.broadcasted_iota(jnp.int32, (tk, tq), 1)

    for p in range(n_pairs):
        qpair = qt_ref[p * LANES:(p + 1) * LANES, :].astype(F32)
        for e in range(2):
            hi, mid, lo = (v.astype(F32) for v in _split3(fq_ref[2 * p + e:2 * p + e + 1, :] * LOG2E))
            top = jnp.where((rowi >= e * dh) & (rowi < (e + 1) * dh), qpair, 0.0)
            bot = jnp.where(rowi == bl * e, hi, jnp.where(rowi == bl * e + 1, mid, jnp.where(
                rowi == bl * e + 2, lo, jnp.where((rowi >= bl * e + 3) & (rowi < bl * e + 6), 1.0, 0.0))))
            qa_scr[e, 0:LANES, :] = top.astype(BF16)
            qa_scr[e, LANES:2 * LANES, :] = bot.astype(BF16)
        m_scr[...] = jnp.full_like(m_scr, -jnp.inf)
        l_scr[...] = jnp.zeros_like(l_scr)
        acc_scr[...] = jnp.zeros_like(acc_scr)

        def scores(kj, slot, c0=0):
            ks = ka_ref[pl.ds(pl.multiple_of(kj * tk, tk), tk), 2 * p * LANES:(2 * p + 2) * LANES]
            for e in range(2):
                st_scr[slot, e, :, c0:] = _dot(ks, qa_scr[e, :, c0:])

        def absorb(kj, slot, diag_off, c0=0):
            for e in range(2):
                st = st_scr[slot, e, :, c0:]
                if diag_off is not None:
                    st = jnp.where((krow + diag_off * tk <= qcol)[:, c0:], st, -jnp.inf)
                m_prev = m_scr[e, :, c0:]
                m_new = jnp.maximum(m_prev, jnp.max(st, axis=0, keepdims=True))
                alpha = jnp.exp2(m_prev - m_new)
                pt = jnp.exp2(st - m_new)
                l_scr[e, :, c0:] = alpha * l_scr[e, :, c0:] + jnp.sum(pt, axis=0, keepdims=True)
                vth = vt_ref[kj, (2 * p + e) * dh:(2 * p + e + 1) * dh, :]
                acc_scr[e, :, c0:] = alpha * acc_scr[e, :, c0:] + _dot(vth, pt.astype(BF16))
                m_scr[e, :, c0:] = m_new

        scores(0, 0)

        def body(jj, carry):
            scores(2 * jj + 1, 1)
            absorb(2 * jj, 0, None)
            scores(2 * jj + 2, 0)
            absorb(2 * jj + 1, 1, None)
            return carry

        lax.fori_loop(0, qi, body, 0)
        scores(2 * qi + 1, 1, tk)
        absorb(2 * qi, 0, 0)
        absorb(2 * qi + 1, 1, 1, tk)
        out_t = jnp.concatenate([acc_scr[e] / l_scr[e] for e in range(2)], axis=0)
        o_ref[:, p * LANES:(p + 1) * LANES] = out_t.T.astype(BF16)


def _fox_prompt(qt, ftc, kaug, vt, *, tq, dh):
    batch, db, s_len = qt.shape
    nk, tk = vt.shape[1], vt.shape[3]
    nq = s_len // tq
    kern = functools.partial(_fox_prompt_kernel, tq=tq, tk=tk, dh=dh)
    return pl.pallas_call(
        kern, grid=(batch, nq),
        in_specs=[pl.BlockSpec((None, db, tq), lambda b, i: (b, 0, i)),
                  pl.BlockSpec((None, ftc.shape[1], tq), lambda b, i: (b, 0, i)),
                  pl.BlockSpec((s_len, kaug.shape[1]), lambda b, i: (b, 0)),
                  pl.BlockSpec((None, nk, db, tk), lambda b, i: (b, 0, 0, 0))],
        out_specs=pl.BlockSpec((tq, db), lambda b, i: (b * nq + i, 0)),
        out_shape=jax.ShapeDtypeStruct((batch * s_len, db), BF16),
        scratch_shapes=[pltpu.VMEM((2, 2 * LANES, tq), BF16), pltpu.VMEM((2, 2, tk, tq), F32),
                        pltpu.VMEM((2, 1, tq), F32),
                        pltpu.VMEM((2, 1, tq), F32), pltpu.VMEM((2, dh, tq), F32)],
        compiler_params=_params(("arbitrary", "arbitrary")),
        name="fox_prompt",
    )(qt, ftc, kaug, vt)


def _fox_sample_kernel(pt_ref, q_ref, kn_ref, vn_ref, lfn_ref, *refs, n_tok, n_heads, dh, pages):
    del pt_ref
    k_refs = refs[:pages]
    v_refs = refs[pages:2 * pages]
    lf_refs = refs[2 * pages:3 * pages]
    o_ref, qs_scr, m_scr, l_scr, acc_scr, r_scr, cq_scr = refs[3 * pages:]
    j = pl.program_id(1)
    nrow = n_tok * n_heads
    dbw = q_ref.shape[-1]
    psz = kn_ref.shape[0]
    hrow = lax.broadcasted_iota(jnp.int32, (n_heads, dbw), 0)
    hcol = lax.broadcasted_iota(jnp.int32, (n_heads, dbw), 1)
    headmask = (hcol >= hrow * dh) & (hcol < (hrow + 1) * dh)
    r2 = lax.broadcasted_iota(jnp.int32, (psz, psz), 0)
    c2 = lax.broadcasted_iota(jnp.int32, (psz, psz), 1)

    def update(s, pv_fn):
        m_prev = m_scr[...]
        m_new = jnp.maximum(m_prev, jnp.max(s, axis=-1, keepdims=True))
        alpha = jnp.exp(m_prev - m_new)
        pm = jnp.exp(s - m_new)
        l_scr[...] = alpha * l_scr[...] + jnp.sum(pm, axis=-1, keepdims=True)
        acc_scr[...] = alpha * acc_scr[...] + pv_fn(pm.astype(BF16))
        m_scr[...] = m_new

    @pl.when(j == 0)
    def _():
        q = q_ref[...]
        qs = jnp.concatenate(
            [jnp.where(headmask, jnp.broadcast_to(q[t:t + 1, :], (n_heads, dbw)), 0.0)
             for t in range(n_tok)], axis=0).astype(BF16)
        qs_scr[...] = qs
        triu = jnp.where(r2 <= c2, 1.0, 0.0).astype(BF16)
        cn = _dot_exact_r(lfn_ref[...], triu)
        cq = jnp.concatenate([cn[:, t:t + 1] for t in range(n_tok)], axis=0)
        cq_scr[...] = cq
        bias = cq - jnp.concatenate([cn] * n_tok, axis=0)
        s = _dot_nt(qs, kn_ref[...]) + bias
        trow = lax.broadcasted_iota(jnp.int32, (nrow, psz), 0) // n_heads
        scol = lax.broadcasted_iota(jnp.int32, (nrow, psz), 1)
        s = jnp.where(scol <= trow, s, -jnp.inf)
        m_scr[...] = jnp.full_like(m_scr, -jnp.inf)
        l_scr[...] = jnp.zeros_like(l_scr)
        acc_scr[...] = jnp.zeros_like(acc_scr)
        r_scr[...] = jnp.zeros_like(r_scr)
        update(s, lambda pm: _dot(pm, vn_ref[...]))

    qs = qs_scr[...]
    cq = cq_scr[...]
    later = jnp.where(r2 > c2, 1.0, 0.0).astype(BF16)
    lf_all = jnp.concatenate([lf_refs[i][...] for i in range(pages)], axis=0)
    suf_all = _dot_exact_r(lf_all, later)
    tot_all = jnp.sum(lf_all, axis=-1, keepdims=True)
    rsum = r_scr[:, 0:1]
    parts = []
    for i in range(pages):
        suf = suf_all[i * n_heads:(i + 1) * n_heads, :] + rsum
        parts.append(_dot(qs, k_refs[i][...].astype(BF16)) + (cq + jnp.concatenate([suf] * n_tok, axis=0)))
        rsum = rsum + tot_all[i * n_heads:(i + 1) * n_heads, :]
    r_scr[:, 0:1] = rsum

    def pv_pages(pm):
        acc = None
        for i in range(pages):
            term = _dot_nt(pm[:, i * psz:(i + 1) * psz], v_refs[i][...].astype(BF16))
            acc = term if acc is None else acc + term
        return acc

    update(jnp.concatenate(parts, axis=1), pv_pages)

    @pl.when(j == pl.num_programs(1) - 1)
    def _():
        out = acc_scr[...] / l_scr[...]
        for t in range(n_tok):
            blk = jnp.where(headmask, out[t * n_heads:(t + 1) * n_heads, :], 0.0)
            o_ref[t:t + 1, :] = jnp.sum(blk, axis=0, keepdims=True).astype(o_ref.dtype)


def _fox_sample(page_table, q, k_new, v_new, lft_new, cache_kt, cache_vt, cache_lft, layer, *, n_heads, dh, pages):
    bd, n_tok, dbw = q.shape
    n_pages = page_table.shape[1]
    psz = cache_kt.shape[-1]
    steps = n_pages // pages
    nrow = n_tok * n_heads

    def page_map(i):
        return lambda b, j, pt: (layer, pt[b, n_pages - 1 - (j * pages + i)], 0, 0)

    per_b = lambda shape: pl.BlockSpec((None,) + shape, lambda b, j, pt: (b, 0, 0))
    in_specs = [per_b((n_tok, dbw)), per_b((psz, dbw)), per_b((psz, dbw)), per_b((n_heads, psz))]
    in_specs += [pl.BlockSpec((None, None, dbw, psz), page_map(i)) for i in range(pages)]
    in_specs += [pl.BlockSpec((None, None, dbw, psz), page_map(i)) for i in range(pages)]
    in_specs += [pl.BlockSpec((None, None, n_heads, psz), page_map(i)) for i in range(pages)]
    kern = functools.partial(_fox_sample_kernel, n_tok=n_tok, n_heads=n_heads, dh=dh, pages=pages)
    grid_spec = pltpu.PrefetchScalarGridSpec(
        num_scalar_prefetch=1, grid=(bd, steps), in_specs=in_specs,
        out_specs=pl.BlockSpec((None, n_tok, dbw), lambda b, j, pt: (b, 0, 0)),
        scratch_shapes=[pltpu.VMEM((nrow, dbw), BF16), pltpu.VMEM((nrow, 1), F32), pltpu.VMEM((nrow, 1), F32),
                        pltpu.VMEM((nrow, dbw), F32), pltpu.VMEM((n_heads, LANES), F32),
                        pltpu.VMEM((nrow, 1), F32)])
    return pl.pallas_call(
        kern, grid_spec=grid_spec, out_shape=jax.ShapeDtypeStruct((bd, n_tok, dbw), F32),
        compiler_params=_params(("arbitrary", "arbitrary")), name="fox_sample",
    )(page_table, q, k_new, v_new, lft_new, *([cache_kt] * pages), *([cache_vt] * pages), *([cache_lft] * pages))


def _mlstm_kernel(q_ref, kt_ref, v_ref, sg_ref, c0_ref, n0_ref, m0_ref, h_ref, st_ref, mo_ref, st_scr, m_scr,
                  *, n_heads, dk, dv, valid, ic_col, lf_col):
    c = pl.program_id(1)
    L = q_ref.shape[0]
    dqk = n_heads * dk

    @pl.when(c == 0)
    def _():
        st_scr[:, :dv] = c0_ref[...]
        st_scr[:, dv:] = jnp.broadcast_to(n0_ref[...], (dqk, st_scr.shape[1] - dv))
        for hh in range(n_heads):
            m_scr[hh] = jnp.broadcast_to(m0_ref[hh:hh + 1, 0:1], (L, L))

    q = q_ref[...]
    kt = _split2(kt_ref[...])
    sg = sg_ref[...]
    row = lax.broadcasted_iota(jnp.int32, (L, L), 0)
    col = lax.broadcasted_iota(jnp.int32, (L, L), 1)
    tril = jnp.where(col <= row, 1.0, 0.0).astype(BF16)
    ones = jnp.ones((L, L), BF16)
    eye = col == row
    mask = (col <= row) & (col < valid)
    qlane = lax.broadcasted_iota(jnp.int32, (L, dqk), 1)
    st_pair = _split2(st_scr[...])

    wide = lambda f: jnp.concatenate([f(hh) for hh in range(n_heads)], axis=1)
    lf_w = wide(lambda hh: jnp.broadcast_to(sg[:, lf_col + hh:lf_col + hh + 1], (L, L)))
    ig_w = wide(lambda hh: jnp.broadcast_to(sg[:, ic_col + hh:ic_col + hh + 1], (L, L)))
    b_w = _dot_exact_l(tril, lf_w)
    a_w = ig_w - b_w
    a_rw = _dot_exact_l(ones, jnp.where(wide(lambda hh: eye), ig_w, 0.0)
                        - jnp.where(wide(lambda hh: col >= row), lf_w, 0.0))
    qms = [_split2(jnp.where((qlane >= hh * dk) & (qlane < (hh + 1) * dk), q, 0.0)) for hh in range(n_heads)]
    qk_all = [_dot_hilo(qm, kt) for qm in qms]
    qcn_all = [_dot_hilo(qm, st_pair) for qm in qms]

    for hh in range(n_heads):
        kth = tuple(part[hh * dk:(hh + 1) * dk, :] for part in kt)
        vh = v_ref[:, hh * dv:(hh + 1) * dv]
        b_c = b_w[:, hh * L:(hh + 1) * L]
        a_c = a_w[:, hh * L:(hh + 1) * L]
        a_r = a_rw[:, hh * L:(hh + 1) * L]
        qcn = qcn_all[hh]
        m_old = m_scr[hh]
        dmat = jnp.where(mask, b_c + a_r, -jnp.inf)
        inter = b_c + m_old
        m_t = jnp.maximum(jnp.max(dmat, axis=-1, keepdims=True), inter)
        w_intra = jnp.exp(dmat - m_t) * qk_all[hh]
        w_inter = jnp.exp(inter - m_t)
        num = _dot_hilo(_split2(w_intra), _split2(vh)) + w_inter[:, 0:1] * qcn[:, :dv]
        den = jnp.sum(w_intra, axis=-1, keepdims=True) + w_inter[:, 0:1] * qcn[:, dv:dv + 1]
        hout = num / jnp.maximum(jnp.abs(den), jnp.exp(-m_t[:, 0:1]))
        h_ref[:, hh * dv:(hh + 1) * dv] = hout

        m_new = jnp.broadcast_to(m_t[valid - 1:valid, :], (L, L))
        b_last = jnp.broadcast_to(b_c[valid - 1:valid, :], (L, L))
        w_k = jnp.where(row < valid, jnp.exp(b_last + a_c - m_new), 0.0)
        decay = jnp.exp(b_last + m_old - m_new)[0:1, 0:1]
        rhs = jnp.concatenate([w_k[:, 0:1] * vh, jnp.broadcast_to(w_k[:, 0:1], (L, st_scr.shape[1] - dv))],
                              axis=1)
        upd = _dot_hilo(kth, _split2(rhs))
        st_scr[hh * dk:(hh + 1) * dk, :] = decay * st_scr[hh * dk:(hh + 1) * dk, :] + upd
        m_scr[hh] = m_new

    @pl.when(c == pl.num_programs(1) - 1)
    def _():
        st_ref[...] = st_scr[...]
        for hh in range(n_heads):
            mo_ref[hh:hh + 1, :] = m_scr[hh][0:1, :]


def _mlstm(q, kt, v, sg, c0t, n0, m0, *, n_heads, dk, dv, valid, ic_col, lf_col):
    batch, s_len, dqk = q.shape
    L = MLSTM_CHUNK
    nc = s_len // L
    wst = 2 * dv
    kern = functools.partial(_mlstm_kernel, n_heads=n_heads, dk=dk, dv=dv, valid=valid, ic_col=ic_col, lf_col=lf_col)
    return pl.pallas_call(
        kern, grid=(batch, nc),
        in_specs=[pl.BlockSpec((None, L, dqk), lambda b, c: (b, c, 0)),
                  pl.BlockSpec((None, dqk, L), lambda b, c: (b, 0, c)),
                  pl.BlockSpec((None, L, n_heads * dv), lambda b, c: (b, c, 0)),
                  pl.BlockSpec((None, L, sg.shape[2]), lambda b, c: (b, c, 0)),
                  pl.BlockSpec((None, dqk, dv), lambda b, c: (b, 0, 0)),
                  pl.BlockSpec((None, dqk, 1), lambda b, c: (b, 0, 0)),
                  pl.BlockSpec((None, n_heads, 1), lambda b, c: (b, 0, 0))],
        out_specs=[pl.BlockSpec((None, L, n_heads * dv), lambda b, c: (b, c, 0)),
                   pl.BlockSpec((None, dqk, wst), lambda b, c: (b, 0, 0)),
                   pl.BlockSpec((None, n_heads, L), lambda b, c: (b, 0, 0))],
        out_shape=[jax.ShapeDtypeStruct((batch, s_len, n_heads * dv), F32),
                   jax.ShapeDtypeStruct((batch, dqk, wst), F32),
                   jax.ShapeDtypeStruct((batch, n_heads, L), F32)],
        scratch_shapes=[pltpu.VMEM((dqk, wst), F32), pltpu.VMEM((n_heads, L, L), F32)],
        compiler_params=_params(("arbitrary", "arbitrary")),
        name="mlstm",
    )(q, kt, v, sg, c0t, n0, m0)


def _merge_kernel(x_ref, mod_ref, g_ref, ya_ref, yb_ref, hc_ref,
                  woc_ref, wgt_ref, wbr_ref, wout_ref, o_ref):
    x = x_ref[...]
    h = _modnorm(x, g_ref[...], mod_ref[0], mod_ref[1]).astype(BF16)
    oc = jax.nn.sigmoid(_dot(h, woc_ref[...]))
    ys = (ya_ref[...], yb_ref[...], (oc * hc_ref[...]).astype(BF16))
    d = x.shape[1]
    acc = None
    for nb in range(len(ys)):
        gate = jax.nn.sigmoid(_dot(h, wgt_ref[:, nb * d:(nb + 1) * d]))
        term = gate * _dot(ys[nb], wbr_ref[nb])
        acc = term if acc is None else acc + term
    o_ref[...] = x + mod_ref[2] * _dot(acc.astype(BF16), wout_ref[...])


def _merge(x, mod, ya, yb, hc, mw, *, tm, tiles_per_group):
    m, d = x.shape
    modspec, gspec = _mod_specs(mod, tm, lambda i: (i // tiles_per_group, i))
    full = lambda a: pl.BlockSpec(a.shape, lambda i: (0,) * a.ndim, pipeline_mode=pl.Buffered(1))
    rowspec = lambda n: pl.BlockSpec((tm, n), lambda i: (i, 0))
    ws = [mw["woc"], mw["wgt"], mw["wbr"], mw["wout"]]
    return pl.pallas_call(
        _merge_kernel, grid=(m // tm,),
        in_specs=[rowspec(d), modspec, gspec, rowspec(ya.shape[1]), rowspec(yb.shape[1]),
                  rowspec(hc.shape[1])] + [full(w) for w in ws],
        out_specs=rowspec(d),
        out_shape=jax.ShapeDtypeStruct((m, d), F32),
        compiler_params=_params(("arbitrary",)),
        name="merge",
    )(x, mod[0], mod[1], ya, yb, hc, *ws)


def _pick_tile(n, pref):
    t = min(n, pref)
    while n % t:
        t //= 2
    return t


def kernel(x_prompt, x_sample, cache_k, cache_v, cache_logf, state_C, state_n, state_m, page_table,
           c_prompt, c_sample, w_ada, b_ada, g_norm, w_ff_up, w_ff_down, w_in, g_va, w_s, b_s,
           g_qb, g_kb, b_fb, b_ic, b_fc, w_branch, w_out):
    B, S, D = x_prompt.shape
    Bd, T, _ = x_sample.shape
    depth = w_ada.shape[0]
    n_pool, psz, HB, dh = cache_k.shape[1:]
    HC, DV, DK = state_C.shape[2:]
    GA, chunk_a = w_s.shape[1], w_s.shape[2]
    DA = g_va.shape[1]
    DB = HB * dh
    DQK = HC * DK
    DC = HC * DV
    NBR, BW = w_branch.shape[1], w_branch.shape[2]
    n_sub = g_norm.shape[1]
    Ms = Bd * T

    splits = (DA, DA, DB, DB, DB, HB, DQK, DQK, DC, HC, HC, DC, NBR * D)
    offs = [0]
    for sz in splits:
        offs.append(offs[-1] + sz)
    (o_ua, o_va, o_qb, o_kb, o_vb, o_fb, o_qc, o_kc, o_vc, o_ic, o_fc, o_oc, o_gt, o_end) = offs

    w_in_t = jnp.swapaxes(w_in, 1, 2)
    nsg = LANES
    nsgt = 2 * ((HB + 2 * HC + 15) // 16) * 8
    n_small = HB + 2 * HC

    def cols(l, a, b):
        return jnp.swapaxes(w_in_t[l, a:b], 0, 1).astype(BF16)

    def rows(l, a, b):
        return w_in_t[l, a:b].astype(BF16)

    w_up16 = w_ff_up.astype(BF16)
    w_down16 = w_ff_down.astype(BF16)
    w_br16 = w_branch.astype(BF16)
    w_out16 = w_out.astype(BF16)

    sel_np = np.zeros((nsg, (HB // 2) * LANES), np.float32)
    for hd in range(HB):
        base = (hd // 2) * LANES + (hd % 2) * BIAS_LANES_PER_HEAD + BIAS_LANES_PER_HEAD // 2
        sel_np[hd, base:base + 3] = 1.0
    sel = jnp.asarray(sel_np, BF16)

    layer_w = []
    for l in range(depth):
        small_rows = jnp.concatenate([w_in_t[l, o_fb:o_qc], w_in_t[l, o_ic:o_oc]], axis=0)
        wst = jnp.zeros((nsgt, D), F32).at[:n_small].set(small_rows).astype(BF16)
        ws = jnp.zeros((D, nsg), F32).at[:, :n_small].set(small_rows.T).astype(BF16)
        bias_small = jnp.concatenate([b_fb[l], b_ic[l], b_fc[l]])
        bsg = jnp.zeros((1, nsg), F32).at[0, :n_small].set(bias_small)
        bsgt = jnp.zeros((nsgt, 1), F32).at[:n_small, 0].set(bias_small)
        common = dict(
            wa=cols(l, o_ua, o_qb), wk=cols(l, o_kb, o_vb),
            wc=jnp.concatenate([cols(l, o_qc, o_kc), cols(l, o_vc, o_ic)], axis=1), dvc=DC,
            ws=ws, gva=g_va[l].reshape(1, DA), gk_row=jnp.tile(g_kb[l], HB).reshape(1, DB), bsg=bsg)
        wblk = jnp.einsum("ab,gts->gatbs", jnp.eye(Bd, dtype=F32), w_s[l][:, :T, :T]).reshape(GA, Ms, Ms)
        pw_p = dict(common, wqt=rows(l, o_qb, o_kb), wkt=rows(l, o_kb, o_vb), wvt=rows(l, o_vb, o_fb),
                    wkct=rows(l, o_kc, o_vc), wst=wst, gq_col=g_qb[l].reshape(dh, 1), gk_col=g_kb[l].reshape(dh, 1),
                    bsgt=bsgt, sel=sel, wmix=w_s[l], bmix=b_s[l].T)
        pw_s = dict(common, wq=cols(l, o_qb, o_kb), wv=cols(l, o_vb, o_fb), wkc=cols(l, o_kc, o_vc),
                    gq_row=jnp.tile(g_qb[l], HB).reshape(1, DB), wmix=wblk,
                    bmix=jnp.tile(b_s[l][:, :T], (1, Bd)).T)
        mw = dict(woc=cols(l, o_oc, o_gt), wgt=cols(l, o_gt, o_end), wbr=w_br16[l], wout=w_out16[l])
        layer_w.append((pw_p, pw_s, mw))

    r_all = B + Bd
    r_pad = -(-r_all // 8) * 8
    c_all = jnp.zeros((r_pad, D), F32).at[:B].set(c_prompt).at[B:r_all].set(c_sample)
    mods = _ada_mod(c_all, w_ada, b_ada).reshape(depth, r_pad, n_sub, 3, D)
    mods_p = jnp.transpose(mods[:, :B], (0, 2, 1, 3, 4)).reshape(depth, n_sub, B, 3, 1, D)
    mods_s = jnp.transpose(jnp.repeat(mods[:, B:r_all], T, axis=1), (0, 2, 3, 1, 4))
    gains = g_norm.reshape(depth, n_sub, 1, D)

    cache_kt = jnp.transpose(cache_k, (0, 1, 3, 4, 2)).reshape(depth, n_pool, DB, psz)
    cache_vt = jnp.transpose(cache_v, (0, 1, 3, 4, 2)).reshape(depth, n_pool, DB, psz)
    cache_lft = jnp.transpose(cache_logf, (0, 1, 3, 2))

    tm_p = _pick_tile(S, 512)
    tm_proj = _pick_tile(S, 512)
    tk = _pick_tile(S, 256)
    tq = 2 * tk
    pages = _pick_tile(page_table.shape[1], 32)

    xp = x_prompt.reshape(B * S, D)
    xs = x_sample.reshape(Ms, D)
    outs_p, outs_s, chunk_v = [], [], []
    pr = None
    for l in range(depth):
        pw_p, pw_s, mw = layer_w[l]

        mod_p = lambda sub: (mods_p, gains, l, sub)
        mod_s = lambda sub: (mods_s, gains, l, sub)

        xp = _ffn(xp, mod_p(0), w_up16, w_down16, l, 0, tm_p, S // tm_p)
        pr = _proj(xp, mod_p(1), pw_p, prompt=True, batch=B, tm=tm_proj, tk=tk, chunk=chunk_a, groups=GA,
                   dh=dh, n_lf=HB, n_ic=HC, stacked=(l, depth, pr))
        yb =_fox_prompt(pr["qt16"], pr["ftc"], pr["kaug"], pr["vt16"], tq=tq, dh=dh)
        hc, st, mo = _mlstm(pr["qc"].reshape(B, S, DQK), pr["kct"], pr["vc"].reshape(B, S, DC),
                            pr["sg"].reshape(B, S, nsg),
                            jnp.zeros((B, DQK, DV), F32), jnp.zeros((B, DQK, 1), F32), jnp.zeros((B, HC, 1), F32),
                            n_heads=HC, dk=DK, dv=DV, valid=MLSTM_CHUNK, ic_col=HB, lf_col=HB + HC)
        xp = _merge(xp, mod_p(1), pr["ya"], yb, hc.reshape(B * S, DC), mw, tm=tm_proj,
                    tiles_per_group=S // tm_proj)
        xp = _ffn(xp, mod_p(2), w_up16, w_down16, l, 1, tm_p, S // tm_p)
        outs_p.append((
            jnp.swapaxes(st[:, :, :DV].reshape(B, HC, DK, DV), -1, -2),
            st[:, :, DV].reshape(B, HC, DK),
            mo[:, :, 0]))

        xs = _ffn(xs, mod_s(0), w_up16, w_down16, l, 0, Ms, 1)
        ps = _proj(xs, mod_s(1), pw_s, prompt=False, batch=1, tm=Ms, tk=Ms, chunk=Ms, groups=GA, dh=dh,
                   n_lf=HB, n_ic=HC)
        k32_s, v32_s, sg_s = ps["k32"], ps["v32"], ps["sg"]
        pad_t = lambda a: jnp.pad(a.reshape(Bd, T, -1), ((0, 0), (0, psz - T), (0, 0)))
        lft_new = jnp.swapaxes(pad_t(sg_s[:, :HB]), 1, 2)
        yb_s = _fox_sample(page_table, ps["q32"].reshape(Bd, T, DB), pad_t(k32_s).astype(BF16),
                           pad_t(v32_s).astype(BF16), lft_new, cache_kt, cache_vt, cache_lft, l,
                           n_heads=HB, dh=dh, pages=pages)
        padc = lambda a: jnp.pad(a.reshape(Bd, T, -1), ((0, 0), (0, MLSTM_CHUNK - T), (0, 0)))
        hc_s, st_s, mo_s = _mlstm(
            padc(ps["qc"]), jnp.swapaxes(padc(ps["kc"]), 1, 2), padc(ps["vc"]), padc(sg_s),
            jnp.swapaxes(state_C[l], -1, -2).reshape(Bd, DQK, DV), state_n[l].reshape(Bd, DQK, 1),
            state_m[l].reshape(Bd, HC, 1), n_heads=HC, dk=DK, dv=DV, valid=T, ic_col=HB, lf_col=HB + HC)
        xs = _merge(xs, mod_s(1), ps["ya"], yb_s.reshape(Ms, DB).astype(BF16), hc_s[:, :T].reshape(Ms, DC),
                    mw, tm=Ms, tiles_per_group=1)
        xs = _ffn(xs, mod_s(2), w_up16, w_down16, l, 1, Ms, 1)
        outs_s.append((
            k32_s.reshape(Bd, T, HB, dh), v32_s.reshape(Bd, T, HB, dh), sg_s[:, :HB].reshape(Bd, T, HB),
            jnp.swapaxes(st_s[:, :, :DV].reshape(Bd, HC, DK, DV), -1, -2),
            st_s[:, :, DV].reshape(Bd, HC, DK),
            mo_s[:, :, 0]))
        chunk_v.append(ps["va"].reshape(Bd, T, DA))

    stk = lambda states, i: jnp.stack([s[i] for s in states])
    return (xp.reshape(B, S, D), xs.reshape(Bd, T, D),
            jnp.transpose(pr["kt32"].reshape(depth, B, HB, dh, S), (0, 1, 4, 2, 3)),
            jnp.transpose(pr["vt32"].reshape(depth, B, HB, dh, S), (0, 1, 4, 2, 3)),
            jnp.transpose(pr["sgt"][:, :, :HB, :], (0, 1, 3, 2)),
            stk(outs_p, 0), stk(outs_p, 1), stk(outs_p, 2),
            stk(outs_s, 0), stk(outs_s, 1), stk(outs_s, 2), stk(outs_s, 3), stk(outs_s, 4), stk(outs_s, 5),
            jnp.stack(chunk_v))
```

```python
import functools

import numpy as np
import jax
import jax.numpy as jnp
from jax import lax
from jax.experimental import pallas as pl
from jax.experimental.pallas import tpu as pltpu

F32 = jnp.float32
BF16 = jnp.bfloat16
EPS = 1e-6
FFN_RES = 0.5
LANES = 128
LOG2E = 1.4426950408889634
MLSTM_CHUNK = 128
VMEM_LIMIT = 56 * 1024 * 1024
NT_DIMS = (((1,), (1,)), ((), ()))


def _dot(a, b):
    return jnp.dot(a, b, preferred_element_type=F32)


def _dot_nt(a, b):
    return lax.dot_general(a, b, NT_DIMS, preferred_element_type=F32)


def _split3(x):
    hi = x.astype(BF16)
    r = x - hi.astype(F32)
    mid = r.astype(BF16)
    lo = (r - mid.astype(F32)).astype(BF16)
    return hi, mid, lo


def _split2(x):
    hi = x.astype(BF16)
    return hi, (x - hi.astype(F32)).astype(BF16)


def _dot_hilo(a, b):
    return _dot(a[0], b[0]) + (_dot(a[0], b[1]) + _dot(a[1], b[0]))


def _dot_exact_l(a, x):
    hi, mid, lo = _split3(x)
    return _dot(a, hi) + _dot(a, mid) + _dot(a, lo)


def _dot_exact_r(x, a):
    hi, mid, lo = _split3(x)
    return _dot(hi, a) + _dot(mid, a) + _dot(lo, a)


def _modnorm(x, g, shift, scale):
    y = x * lax.rsqrt(jnp.mean(x * x, axis=-1, keepdims=True) + EPS)
    return (y * g) * (1 + scale) + shift


def _params(sem):
    return pltpu.CompilerParams(dimension_semantics=sem, vmem_limit_bytes=VMEM_LIMIT)


def _ada_kernel(c_ref, w_ref, b_ref, o_ref):
    s = jax.nn.silu(c_ref[...]).astype(BF16)
    o_ref[...] = _dot(s, w_ref[...].astype(BF16)) + b_ref[...]


def _ada_mod(c_all, w_ada, b_ada):
    depth, d, n = w_ada.shape
    r = c_all.shape[0]
    tn = n // 8
    return pl.pallas_call(
        _ada_kernel,
        grid=(depth, n // tn),
        in_specs=[pl.BlockSpec((r, d), lambda l, j: (0, 0)),
                  pl.BlockSpec((None, d, tn), lambda l, j: (l, 0, j)),
                  pl.BlockSpec((None, 1, tn), lambda l, j: (l, 0, j))],
        out_specs=pl.BlockSpec((None, r, tn), lambda l, j: (l, 0, j)),
        out_shape=jax.ShapeDtypeStruct((depth, r, n), F32),
        compiler_params=_params(("arbitrary", "arbitrary")),
        name="ada_mod",
    )(c_all, w_ada, b_ada.reshape(depth, 1, n))


def _mod_specs(mod, tm, tile_index):
    arr, gain, l, sub = mod
    d = arr.shape[-1]
    if arr.ndim == 6:
        mspec = pl.BlockSpec((None, None, None, 3, 1, d), lambda *g: (l, sub, tile_index(*g)[0], 0, 0, 0))
    else:
        mspec = pl.BlockSpec((None, None, 3, tm, d), lambda *g: (l, sub, 0, tile_index(*g)[1], 0))
    return mspec, pl.BlockSpec((None, None, 1, d), lambda *g: (l, sub, 0, 0))


def _ffn_kernel(x_ref, mod_ref, g_ref, wu_ref, wd_ref, o_ref, *, n_chunks):
    x = x_ref[...]
    h = _modnorm(x, g_ref[...], mod_ref[0], mod_ref[1]).astype(BF16)
    dff = wd_ref.shape[0]
    tf = dff // n_chunks
    acc = None
    for j in range(n_chunks):
        a = _dot(h, wu_ref[:, j * tf:(j + 1) * tf])
        b = _dot(h, wu_ref[:, dff + j * tf:dff + (j + 1) * tf])
        act = (jax.nn.silu(a) * b).astype(BF16)
        term = _dot(act, wd_ref[j * tf:(j + 1) * tf, :])
        acc = term if acc is None else acc + term
    o_ref[...] = x + (FFN_RES * mod_ref[2]) * acc


def _ffn(x, mod, w_up, w_down, l, k, tm, tiles_per_group):
    m, d = x.shape
    dff = w_down.shape[2]
    n_chunks = 2 if (dff // 2) % LANES == 0 else 1
    modspec, gspec = _mod_specs(mod, tm, lambda i: (i // tiles_per_group, i))
    resident = lambda shape: pl.BlockSpec((None, None) + shape, lambda i: (l, k, 0, 0), pipeline_mode=pl.Buffered(1))
    return pl.pallas_call(
        functools.partial(_ffn_kernel, n_chunks=n_chunks),
        grid=(m // tm,),
        in_specs=[pl.BlockSpec((tm, d), lambda i: (i, 0)), modspec, gspec,
                  resident((d, 2 * dff)), resident((dff, d))],
        out_specs=pl.BlockSpec((tm, d), lambda i: (i, 0)),
        out_shape=jax.ShapeDtypeStruct((m, d), F32),
        compiler_params=_params(("arbitrary",)),
        name="ffn",
    )(x, mod[0], mod[1], w_up, w_down)


def _group_rms_lanes(z, gsz):
    n = z.shape[-1]
    zz = z * z
    parts = []
    lane = lax.broadcasted_iota(jnp.int32, (z.shape[0], LANES), 1)
    for p in range(n // LANES):
        blk = zz[:, p * LANES:(p + 1) * LANES]
        scale = jnp.zeros_like(blk)
        for r in range(LANES // gsz):
            sel = (lane >= r * gsz) & (lane < (r + 1) * gsz)
            ms = jnp.sum(jnp.where(sel, blk, 0.0), axis=-1, keepdims=True) / gsz
            scale = jnp.where(sel, lax.rsqrt(ms + EPS), scale)
        parts.append(scale)
    return jnp.concatenate(parts, axis=-1)


def _rms_rows(zt, dh, gcol):
    outs = []
    for hh in range(zt.shape[0] // dh):
        blk = zt[hh * dh:(hh + 1) * dh, :]
        ms = jnp.mean(blk * blk, axis=0, keepdims=True)
        outs.append((blk * lax.rsqrt(ms + EPS)) * gcol)
    return outs


BIAS_LANES_PER_HEAD = 6
STACKED_OUTPUTS = ("kt32", "vt32", "sgt")


def _proj_kernel(*refs, names, prompt, tm, tk, chunk, groups, dh, n_lf, n_ic):
    r = dict(zip(names, refs))
    h = _modnorm(r["x"][...], r["g"][...], r["mod"][0], r["mod"][1]).astype(BF16)

    za = _dot(h, r["wa"][...])

    def mixer_a():
        da = za.shape[1] // 2
        dg = da // groups
        ua = jax.nn.gelu(za[:, :da])
        vg = jax.nn.gelu(za[:, da:])
        va = (vg * lax.rsqrt(jnp.mean(vg * vg, axis=-1, keepdims=True) + EPS)) * r["gva"][...]
        if "va" in r:
            r["va"][...] = va
        vab = va.astype(BF16)
        rr = lax.broadcasted_iota(jnp.int32, (chunk, chunk), 0)
        cc = lax.broadcasted_iota(jnp.int32, (chunk, chunk), 1)
        for gi in range(groups):
            w = jnp.where(cc <= rr, r["wmix"][gi], 0.0).astype(BF16)
            bcol = r["bmix"][:, gi:gi + 1]
            for c in range(tm // chunk):
                rs = slice(c * chunk, (c + 1) * chunk)
                cs = slice(gi * dg, (gi + 1) * dg)
                mix = _dot(w, vab[rs, cs]) + bcol
                r["ya"][rs, cs] = (ua[rs, cs] * mix).astype(BF16)

    zs = _dot(h, r["ws"][...]) + r["bsg"][...]
    col = lax.broadcasted_iota(jnp.int32, zs.shape, 1)
    lsg = jnp.where((col >= n_lf) & (col < n_lf + n_ic), zs, jax.nn.log_sigmoid(zs))
    r["sg"][...] = lsg

    zc = _dot(h, r["wc"][...])
    dqk = r["qc"].shape[-1]
    dkc = dqk // n_ic
    r["qc"][...] = zc[:, :dqk]
    r["vc"][...] = zc[:, dqk:]

    scale = dh ** -0.5
    zk = _dot(h, r["wk"][...])
    kn = (zk * _group_rms_lanes(zk, dh)) * r["gk_row"][...]
    if not prompt:
        zq = _dot(h, r["wq"][...])
        r["q32"][...] = ((zq * _group_rms_lanes(zq, dh)) * r["gq_row"][...]) * scale
        r["k32"][...] = kn
        r["v32"][...] = _dot(h, r["wv"][...])
        r["kc"][...] = _dot(h, r["wkc"][...]) * (dkc ** -0.5)
        mixer_a()
        return

    for hh, blk in enumerate(_rms_rows(_dot_nt(r["wqt"][...], h), dh, r["gq_col"][...])):
        r["qt16"][hh * dh:(hh + 1) * dh, :] = (blk * (scale * LOG2E)).astype(BF16)
    for hh, blk in enumerate(_rms_rows(_dot_nt(r["wkt"][...], h), dh, r["gk_col"][...])):
        r["kt32"][hh * dh:(hh + 1) * dh, :] = blk
    vt = _dot_nt(r["wvt"][...], h)
    r["vt32"][...] = vt
    for s in range(tm // tk):
        r["vt16"][s] = vt[:, s * tk:(s + 1) * tk].astype(BF16)
    r["kct"][...] = _dot_nt(r["wkct"][...], h) * (dkc ** -0.5)

    carry_r, carry_c = r["carry_r"], r["carry_c"]

    @pl.when(pl.program_id(1) == 0)
    def _():
        carry_r[...] = jnp.zeros_like(carry_r)
        carry_c[...] = jnp.zeros_like(carry_c)

    r2 = lax.broadcasted_iota(jnp.int32, (tm, tm), 0)
    c2 = lax.broadcasted_iota(jnp.int32, (tm, tm), 1)
    tril = jnp.where(c2 <= r2, 1.0, 0.0).astype(BF16)
    fcum = _dot_exact_l(tril, lsg) + carry_r[0:1, :]
    carry_r[0:1, :] = fcum[tm - 1:tm, :]

    zst = _dot_nt(r["wst"][...], h) + r["bsgt"][...]
    row = lax.broadcasted_iota(jnp.int32, zst.shape, 0)
    lsgt = jnp.where((row >= n_lf) & (row < n_lf + n_ic), zst, jax.nn.log_sigmoid(zst))
    r["sgt"][...] = lsgt
    triu = jnp.where(r2 <= c2, 1.0, 0.0).astype(BF16)
    ftc = _dot_exact_r(lsgt, triu) + carry_c[:, 0:1]
    r["ftc"][...] = ftc
    carry_c[:, 0:1] = ftc[:, tm - 1:tm]

    g_all = _dot_exact_r(fcum, r["sel"][...]) * LOG2E
    lane = lax.broadcasted_iota(jnp.int32, (tm, LANES), 1)
    bl = BIAS_LANES_PER_HEAD
    is_one = (lane < bl // 2) | ((lane >= bl) & (lane < bl + bl // 2))
    piece = [(lane == bl // 2 + i) | (lane == bl + bl // 2 + i) for i in range(3)]
    for p in range(kn.shape[1] // LANES):
        hi, mid, lo = (v.astype(F32) for v in _split3(g_all[:, p * LANES:(p + 1) * LANES]))
        fsel = jnp.where(piece[0], hi, jnp.where(piece[1], mid, lo))
        blk = jnp.where(is_one, 1.0, jnp.where(lane < 2 * bl, -fsel, 0.0))
        r["kaug"][:, 2 * p * LANES:(2 * p + 1) * LANES] = kn[:, p * LANES:(p + 1) * LANES].astype(BF16)
        r["kaug"][:, (2 * p + 1) * LANES:(2 * p + 2) * LANES] = blk.astype(BF16)
    mixer_a()


def _proj(x, mod, pw, *, prompt, batch, tm, tk, chunk, groups, dh, n_lf, n_ic, stacked=None):
    m, d = x.shape
    s_len = m // batch
    nt = s_len // tm
    da = pw["wa"].shape[1] // 2
    db = pw["wk"].shape[1]
    dvc = pw["dvc"]
    dqk = pw["wc"].shape[1] - dvc
    nsg = pw["ws"].shape[1]

    def row(i, j):
        return (i * nt + j, 0)

    modspec, gspec = _mod_specs(mod, tm, lambda i, j: (i, i * nt + j))
    full = lambda a: pl.BlockSpec(a.shape, lambda i, j: (0,) * a.ndim, pipeline_mode=pl.Buffered(1))
    rowspec = lambda n: pl.BlockSpec((tm, n), row)
    colspec = lambda n: pl.BlockSpec((None, n, tm), lambda i, j: (i, 0, j))
    slabspec = lambda n: pl.BlockSpec((None, tm // tk, n, tk), lambda i, j: (i, j, 0, 0))
    layer, depth, prev = stacked if stacked is not None else (0, 1, None)
    layerspec = lambda n: pl.BlockSpec((None, None, n, tm), lambda i, j: (layer, i, 0, j))

    ins = [("x", x, rowspec(d)), ("mod", mod[0], modspec), ("g", mod[1], gspec)]
    wnames = (["wa", "wqt", "wk", "wkt", "wvt", "wc", "wkct", "ws", "wst", "gva", "gq_col", "gk_row", "gk_col",
               "bsg", "bsgt", "sel", "wmix", "bmix"] if prompt else
              ["wa", "wq", "wk", "wv", "wc", "wkc", "ws", "gva", "gq_row", "gk_row", "bsg", "wmix", "bmix"])
    ins += [(n, pw[n], full(pw[n])) for n in wnames]

    sds = jax.ShapeDtypeStruct
    if prompt:
        nsgt = pw["wst"].shape[0]
        outs = [("ya", sds((m, da), BF16), rowspec(da)),
                ("qt16", sds((batch, db, s_len), BF16), colspec(db)),
                ("kt32", sds((depth, batch, db, s_len), F32), layerspec(db)),
                ("kaug", sds((m, 2 * db), BF16), rowspec(2 * db)),
                ("vt32", sds((depth, batch, db, s_len), F32), layerspec(db)),
                ("vt16", sds((batch, s_len // tk, db, tk), BF16), slabspec(db)),
                ("qc", sds((m, dqk), F32), rowspec(dqk)),
                ("kct", sds((batch, dqk, s_len), F32), colspec(dqk)),
                ("vc", sds((m, dvc), F32), rowspec(dvc)),
                ("sg", sds((m, nsg), F32), rowspec(nsg)),
                ("sgt", sds((depth, batch, nsgt, s_len), F32), layerspec(nsgt)),
                ("ftc", sds((batch, nsgt, s_len), F32), colspec(nsgt))]
        scratch = [("carry_r", pltpu.VMEM((8, nsg), F32)), ("carry_c", pltpu.VMEM((nsgt, LANES), F32))]
    else:
        outs = [("ya", sds((m, da), BF16), rowspec(da)), ("q32", sds((m, db), F32), rowspec(db)),
                ("k32", sds((m, db), F32), rowspec(db)), ("v32", sds((m, db), F32), rowspec(db)),
                ("qc", sds((m, dqk), F32), rowspec(dqk)), ("kc", sds((m, dqk), F32), rowspec(dqk)),
                ("vc", sds((m, dvc), F32), rowspec(dvc)), ("sg", sds((m, nsg), F32), rowspec(nsg)),
                ("va", sds((m, da), F32), rowspec(da))]
        scratch = []
    aliases = {}
    if prompt and prev is not None:
        out_names = [n for n, _, _ in outs]
        for n in STACKED_OUTPUTS:
            aliases[len(ins)] = out_names.index(n)
            ins.append(("prev_" + n, prev[n], pl.BlockSpec(memory_space=pl.ANY)))
    names = tuple(n for n, _, _ in ins) + tuple(n for n, _, _ in outs) + tuple(n for n, _ in scratch)
    kern = functools.partial(_proj_kernel, names=names, prompt=prompt, tm=tm, tk=tk, chunk=chunk, groups=groups,
                             dh=dh, n_lf=n_lf, n_ic=n_ic)
    res = pl.pallas_call(
        kern, grid=(batch, nt), in_specs=[s for _, _, s in ins], out_specs=[s for _, _, s in outs],
        out_shape=[o for _, o, _ in outs], scratch_shapes=[s for _, s in scratch],
        compiler_params=_params(("arbitrary", "arbitrary")), input_output_aliases=aliases,
        name="proj_prompt" if prompt else "proj_sample",
    )(*[a for _, a, _ in ins])
    return dict(zip([n for n, _, _ in outs], res))


def _fox_prompt_kernel(qt_ref, fq_ref, ka_ref, vt_ref, o_ref, qa_scr, st_scr, m_scr, l_scr, acc_scr, *, tq, tk, dh):
    qi = pl.program_id(1)
    n_pairs = qt_ref.shape[0] // LANES
    assert tq == 2 * tk, "the key-block schedule below pairs blocks: two key blocks per query block"
    bl = BIAS_LANES_PER_HEAD
    rowi = lax.broadcasted_iota(jnp.int32, (LANES, tq), 0)
    krow = lax.broadcasted_iota(jnp.int32, (tk, tq), 0)
    qcol = lax.broadcasted_iota(jnp.int32, (tk, tq), 1)

    for p in range(n_pairs):
        qpair = qt_ref[p * LANES:(p + 1) * LANES, :].astype(F32)
        for e in range(2):
            hi, mid, lo = (v.astype(F32) for v in _split3(fq_ref[2 * p + e:2 * p + e + 1, :] * LOG2E))
            top = jnp.where((rowi >= e * dh) & (rowi < (e + 1) * dh), qpair, 0.0)
            bot = jnp.where(rowi == bl * e, hi, jnp.where(rowi == bl * e + 1, mid, jnp.where(
                rowi == bl * e + 2, lo, jnp.where((rowi >= bl * e + 3) & (rowi < bl * e + 6), 1.0, 0.0))))
            qa_scr[e, 0:LANES, :] = top.astype(BF16)
            qa_scr[e, LANES:2 * LANES, :] = bot.astype(BF16)
        m_scr[...] = jnp.full_like(m_scr, -jnp.inf)
        l_scr[...] = jnp.zeros_like(l_scr)
        acc_scr[...] = jnp.zeros_like(acc_scr)

        def scores(kj, slot, c0=0):
            ks = ka_ref[pl.ds(pl.multiple_of(kj * tk, tk), tk), 2 * p * LANES:(2 * p + 2) * LANES]
            for e in range(2):
                st_scr[slot, e, :, c0:] = _dot(ks, qa_scr[e, :, c0:])

        def absorb(kj, slot, diag_off, c0=0):
            for e in range(2):
                st = st_scr[slot, e, :, c0:]
                if diag_off is not None:
                    st = jnp.where((krow + diag_off * tk <= qcol)[:, c0:], st, -jnp.inf)
                m_prev = m_scr[e, :, c0:]
                m_new = jnp.maximum(m_prev, jnp.max(st, axis=0, keepdims=True))
                alpha = jnp.exp2(m_prev - m_new)
                pt = jnp.exp2(st - m_new)
                l_scr[e, :, c0:] = alpha * l_scr[e, :, c0:] + jnp.sum(pt, axis=0, keepdims=True)
                vth = vt_ref[kj, (2 * p + e) * dh:(2 * p + e + 1) * dh, :]
                acc_scr[e, :, c0:] = alpha * acc_scr[e, :, c0:] + _dot(vth, pt.astype(BF16))
                m_scr[e, :, c0:] = m_new

        scores(0, 0)

        def block_pair(jj):
            scores(2 * jj + 1, 1)
            absorb(2 * jj, 0, None)
            scores(2 * jj + 2, 0)
            absorb(2 * jj + 1, 1, None)

        def body(t, carry):
            block_pair(2 * t)
            block_pair(2 * t + 1)
            return carry

        lax.fori_loop(0, lax.shift_right_logical(qi, 1), body, 0)

        @pl.when(jnp.bitwise_and(qi, 1) == 1)
        def _():
            block_pair(qi - 1)

        scores(2 * qi + 1, 1, tk)
        absorb(2 * qi, 0, 0)
        absorb(2 * qi + 1, 1, 1, tk)
        out_t = jnp.concatenate([acc_scr[e] / l_scr[e] for e in range(2)], axis=0)
        o_ref[:, p * LANES:(p + 1) * LANES] = out_t.T.astype(BF16)


def _fox_prompt(qt, ftc, kaug, vt, *, tq, dh):
    batch, db, s_len = qt.shape
    nk, tk = vt.shape[1], vt.shape[3]
    nq = s_len // tq
    kern = functools.partial(_fox_prompt_kernel, tq=tq, tk=tk, dh=dh)
    return pl.pallas_call(
        kern, grid=(batch, nq),
        in_specs=[pl.BlockSpec((None, db, tq), lambda b, i: (b, 0, i)),
                  pl.BlockSpec((None, ftc.shape[1], tq), lambda b, i: (b, 0, i)),
                  pl.BlockSpec((s_len, kaug.shape[1]), lambda b, i: (b, 0)),
                  pl.BlockSpec((None, nk, db, tk), lambda b, i: (b, 0, 0, 0))],
        out_specs=pl.BlockSpec((tq, db), lambda b, i: (b * nq + i, 0)),
        out_shape=jax.ShapeDtypeStruct((batch * s_len, db), BF16),
        scratch_shapes=[pltpu.VMEM((2, 2 * LANES, tq), BF16), pltpu.VMEM((2, 2, tk, tq), F32),
                        pltpu.VMEM((2, 1, tq), F32),
                        pltpu.VMEM((2, 1, tq), F32), pltpu.VMEM((2, dh, tq), F32)],
        compiler_params=_params(("arbitrary", "arbitrary")),
        name="fox_prompt",
    )(qt, ftc, kaug, vt)


def _fox_sample_kernel(pt_ref, q_ref, kn_ref, vn_ref, lfn_ref, *refs, n_tok, n_heads, dh, pages):
    del pt_ref
    k_refs = refs[:pages]
    v_refs = refs[pages:2 * pages]
    lf_refs = refs[2 * pages:3 * pages]
    o_ref, qs_scr, m_scr, l_scr, acc_scr, r_scr, cq_scr = refs[3 * pages:]
    j = pl.program_id(1)
    nrow = n_tok * n_heads
    dbw = q_ref.shape[-1]
    psz = kn_ref.shape[0]
    hrow = lax.broadcasted_iota(jnp.int32, (n_heads, dbw), 0)
    hcol = lax.broadcasted_iota(jnp.int32, (n_heads, dbw), 1)
    headmask = (hcol >= hrow * dh) & (hcol < (hrow + 1) * dh)
    r2 = lax.broadcasted_iota(jnp.int32, (psz, psz), 0)
    c2 = lax.broadcasted_iota(jnp.int32, (psz, psz), 1)

    def update(s, pv_fn):
        m_prev = m_scr[...]
        m_new = jnp.maximum(m_prev, jnp.max(s, axis=-1, keepdims=True))
        alpha = jnp.exp(m_prev - m_new)
        pm = jnp.exp(s - m_new)
        l_scr[...] = alpha * l_scr[...] + jnp.sum(pm, axis=-1, keepdims=True)
        acc_scr[...] = alpha * acc_scr[...] + pv_fn(pm.astype(BF16))
        m_scr[...] = m_new

    @pl.when(j == 0)
    def _():
        q = q_ref[...]
        qs = jnp.concatenate(
            [jnp.where(headmask, jnp.broadcast_to(q[t:t + 1, :], (n_heads, dbw)), 0.0)
             for t in range(n_tok)], axis=0).astype(BF16)
        qs_scr[...] = qs
        triu = jnp.where(r2 <= c2, 1.0, 0.0).astype(BF16)
        cn = _dot_exact_r(lfn_ref[...], triu)
        cq = jnp.concatenate([cn[:, t:t + 1] for t in range(n_tok)], axis=0)
        cq_scr[...] = cq
        bias = cq - jnp.concatenate([cn] * n_tok, axis=0)
        s = _dot_nt(qs, kn_ref[...]) + bias
        trow = lax.broadcasted_iota(jnp.int32, (nrow, psz), 0) // n_heads
        scol = lax.broadcasted_iota(jnp.int32, (nrow, psz), 1)
        s = jnp.where(scol <= trow, s, -jnp.inf)
        m_scr[...] = jnp.full_like(m_scr, -jnp.inf)
        l_scr[...] = jnp.zeros_like(l_scr)
        acc_scr[...] = jnp.zeros_like(acc_scr)
        r_scr[...] = jnp.zeros_like(r_scr)
        update(s, lambda pm: _dot(pm, vn_ref[...]))

    qs = qs_scr[...]
    cq = cq_scr[...]
    later = jnp.where(r2 > c2, 1.0, 0.0).astype(BF16)
    lf_all = jnp.concatenate([lf_refs[i][...] for i in range(pages)], axis=0)
    suf_all = _dot_exact_r(lf_all, later)
    tot_all = jnp.sum(lf_all, axis=-1, keepdims=True)
    rsum = r_scr[:, 0:1]
    parts = []
    for i in range(pages):
        suf = suf_all[i * n_heads:(i + 1) * n_heads, :] + rsum
        parts.append(_dot(qs, k_refs[i][...].astype(BF16)) + (cq + jnp.concatenate([suf] * n_tok, axis=0)))
        rsum = rsum + tot_all[i * n_heads:(i + 1) * n_heads, :]
    r_scr[:, 0:1] = rsum

    def pv_pages(pm):
        acc = None
        for i in range(pages):
            term = _dot_nt(pm[:, i * psz:(i + 1) * psz], v_refs[i][...].astype(BF16))
            acc = term if acc is None else acc + term
        return acc

    update(jnp.concatenate(parts, axis=1), pv_pages)

    @pl.when(j == pl.num_programs(1) - 1)
    def _():
        out = acc_scr[...] / l_scr[...]
        for t in range(n_tok):
            blk = jnp.where(headmask, out[t * n_heads:(t + 1) * n_heads, :], 0.0)
            o_ref[t:t + 1, :] = jnp.sum(blk, axis=0, keepdims=True).astype(o_ref.dtype)


def _fox_sample(page_table, q, k_new, v_new, lft_new, cache_kt, cache_vt, cache_lft, layer, *, n_heads, dh, pages):
    bd, n_tok, dbw = q.shape
    n_pages = page_table.shape[1]
    psz = cache_kt.shape[-1]
    steps = n_pages // pages
    nrow = n_tok * n_heads

    def page_map(i):
        return lambda b, j, pt: (layer, pt[b, n_pages - 1 - (j * pages + i)], 0, 0)

    per_b = lambda shape: pl.BlockSpec((None,) + shape, lambda b, j, pt: (b, 0, 0))
    in_specs = [per_b((n_tok, dbw)), per_b((psz, dbw)), per_b((psz, dbw)), per_b((n_heads, psz))]
    in_specs += [pl.BlockSpec((None, None, dbw, psz), page_map(i)) for i in range(pages)]
    in_specs += [pl.BlockSpec((None, None, dbw, psz), page_map(i)) for i in range(pages)]
    in_specs += [pl.BlockSpec((None, None, n_heads, psz), page_map(i)) for i in range(pages)]
    kern = functools.partial(_fox_sample_kernel, n_tok=n_tok, n_heads=n_heads, dh=dh, pages=pages)
    grid_spec = pltpu.PrefetchScalarGridSpec(
        num_scalar_prefetch=1, grid=(bd, steps), in_specs=in_specs,
        out_specs=pl.BlockSpec((None, n_tok, dbw), lambda b, j, pt: (b, 0, 0)),
        scratch_shapes=[pltpu.VMEM((nrow, dbw), BF16), pltpu.VMEM((nrow, 1), F32), pltpu.VMEM((nrow, 1), F32),
                        pltpu.VMEM((nrow, dbw), F32), pltpu.VMEM((n_heads, LANES), F32),
                        pltpu.VMEM((nrow, 1), F32)])
    return pl.pallas_call(
        kern, grid_spec=grid_spec, out_shape=jax.ShapeDtypeStruct((bd, n_tok, dbw), F32),
        compiler_params=_params(("arbitrary", "arbitrary")), name="fox_sample",
    )(page_table, q, k_new, v_new, lft_new, *([cache_kt] * pages), *([cache_vt] * pages), *([cache_lft] * pages))


def _mlstm_kernel(q_ref, kt_ref, v_ref, sg_ref, c0_ref, n0_ref, m0_ref, h_ref, st_ref, mo_ref, st_scr, m_scr,
                  *, n_heads, dk, dv, valid, ic_col, lf_col):
    c = pl.program_id(1)
    L = q_ref.shape[0]
    dqk = n_heads * dk

    @pl.when(c == 0)
    def _():
        st_scr[:, :dv] = c0_ref[...]
        st_scr[:, dv:] = jnp.broadcast_to(n0_ref[...], (dqk, st_scr.shape[1] - dv))
        for hh in range(n_heads):
            m_scr[hh] = jnp.broadcast_to(m0_ref[hh:hh + 1, 0:1], (L, L))

    q = q_ref[...]
    kt = _split2(kt_ref[...])
    sg = sg_ref[...]
    row = lax.broadcasted_iota(jnp.int32, (L, L), 0)
    col = lax.broadcasted_iota(jnp.int32, (L, L), 1)
    tril = jnp.where(col <= row, 1.0, 0.0).astype(BF16)
    ones = jnp.ones((L, L), BF16)
    eye = col == row
    mask = (col <= row) & (col < valid)
    qlane = lax.broadcasted_iota(jnp.int32, (L, dqk), 1)
    st_pair = _split2(st_scr[...])

    wide = lambda f: jnp.concatenate([f(hh) for hh in range(n_heads)], axis=1)
    lf_w = wide(lambda hh: jnp.broadcast_to(sg[:, lf_col + hh:lf_col + hh + 1], (L, L)))
    ig_w = wide(lambda hh: jnp.broadcast_to(sg[:, ic_col + hh:ic_col + hh + 1], (L, L)))
    b_w = _dot_exact_l(tril, lf_w)
    a_w = ig_w - b_w
    a_rw = _dot_exact_l(ones, jnp.where(wide(lambda hh: eye), ig_w, 0.0)
                        - jnp.where(wide(lambda hh: col >= row), lf_w, 0.0))
    qms = [_split2(jnp.where((qlane >= hh * dk) & (qlane < (hh + 1) * dk), q, 0.0)) for hh in range(n_heads)]
    qk_all = [_dot_hilo(qm, kt) for qm in qms]
    qcn_all = [_dot_hilo(qm, st_pair) for qm in qms]

    for hh in range(n_heads):
        kth = tuple(part[hh * dk:(hh + 1) * dk, :] for part in kt)
        vh = v_ref[:, hh * dv:(hh + 1) * dv]
        b_c = b_w[:, hh * L:(hh + 1) * L]
        a_c = a_w[:, hh * L:(hh + 1) * L]
        a_r = a_rw[:, hh * L:(hh + 1) * L]
        qcn = qcn_all[hh]
        m_old = m_scr[hh]
        dmat = jnp.where(mask, b_c + a_r, -jnp.inf)
        inter = b_c + m_old
        m_t = jnp.maximum(jnp.max(dmat, axis=-1, keepdims=True), inter)
        w_intra = jnp.exp(dmat - m_t) * qk_all[hh]
        w_inter = jnp.exp(inter - m_t)
        num = _dot_hilo(_split2(w_intra), _split2(vh)) + w_inter[:, 0:1] * qcn[:, :dv]
        den = jnp.sum(w_intra, axis=-1, keepdims=True) + w_inter[:, 0:1] * qcn[:, dv:dv + 1]
        hout = num / jnp.maximum(jnp.abs(den), jnp.exp(-m_t[:, 0:1]))
        h_ref[:, hh * dv:(hh + 1) * dv] = hout

        m_new = jnp.broadcast_to(m_t[valid - 1:valid, :], (L, L))
        b_last = jnp.broadcast_to(b_c[valid - 1:valid, :], (L, L))
        w_k = jnp.where(row < valid, jnp.exp(b_last + a_c - m_new), 0.0)
        decay = jnp.exp(b_last + m_old - m_new)[0:1, 0:1]
        rhs = jnp.concatenate([w_k[:, 0:1] * vh, jnp.broadcast_to(w_k[:, 0:1], (L, st_scr.shape[1] - dv))],
                              axis=1)
        upd = _dot_hilo(kth, _split2(rhs))
        st_scr[hh * dk:(hh + 1) * dk, :] = decay * st_scr[hh * dk:(hh + 1) * dk, :] + upd
        m_scr[hh] = m_new

    @pl.when(c == pl.num_programs(1) - 1)
    def _():
        st_ref[...] = st_scr[...]
        for hh in range(n_heads):
            mo_ref[hh:hh + 1, :] = m_scr[hh][0:1, :]


def _mlstm(q, kt, v, sg, c0t, n0, m0, *, n_heads, dk, dv, valid, ic_col, lf_col):
    batch, s_len, dqk = q.shape
    L = MLSTM_CHUNK
    nc = s_len // L
    wst = 2 * dv
    kern = functools.partial(_mlstm_kernel, n_heads=n_heads, dk=dk, dv=dv, valid=valid, ic_col=ic_col, lf_col=lf_col)
    return pl.pallas_call(
        kern, grid=(batch, nc),
        in_specs=[pl.BlockSpec((None, L, dqk), lambda b, c: (b, c, 0)),
                  pl.BlockSpec((None, dqk, L), lambda b, c: (b, 0, c)),
                  pl.BlockSpec((None, L, n_heads * dv), lambda b, c: (b, c, 0)),
                  pl.BlockSpec((None, L, sg.shape[2]), lambda b, c: (b, c, 0)),
                  pl.BlockSpec((None, dqk, dv), lambda b, c: (b, 0, 0)),
                  pl.BlockSpec((None, dqk, 1), lambda b, c: (b, 0, 0)),
                  pl.BlockSpec((None, n_heads, 1), lambda b, c: (b, 0, 0))],
        out_specs=[pl.BlockSpec((None, L, n_heads * dv), lambda b, c: (b, c, 0)),
                   pl.BlockSpec((None, dqk, wst), lambda b, c: (b, 0, 0)),
                   pl.BlockSpec((None, n_heads, L), lambda b, c: (b, 0, 0))],
        out_shape=[jax.ShapeDtypeStruct((batch, s_len, n_heads * dv), F32),
                   jax.ShapeDtypeStruct((batch, dqk, wst), F32),
                   jax.ShapeDtypeStruct((batch, n_heads, L), F32)],
        scratch_shapes=[pltpu.VMEM((dqk, wst), F32), pltpu.VMEM((n_heads, L, L), F32)],
        compiler_params=_params(("arbitrary", "arbitrary")),
        name="mlstm",
    )(q, kt, v, sg, c0t, n0, m0)


def _merge_kernel(x_ref, mod_ref, g_ref, ya_ref, yb_ref, hc_ref,
                  woc_ref, wgt_ref, wbr_ref, wout_ref, o_ref):
    x = x_ref[...]
    h = _modnorm(x, g_ref[...], mod_ref[0], mod_ref[1]).astype(BF16)
    oc = jax.nn.sigmoid(_dot(h, woc_ref[...]))
    ys = (ya_ref[...], yb_ref[...], (oc * hc_ref[...]).astype(BF16))
    d = x.shape[1]
    acc = None
    for nb in range(len(ys)):
        gate = jax.nn.sigmoid(_dot(h, wgt_ref[:, nb * d:(nb + 1) * d]))
        term = gate * _dot(ys[nb], wbr_ref[nb])
        acc = term if acc is None else acc + term
    o_ref[...] = x + mod_ref[2] * _dot(acc.astype(BF16), wout_ref[...])


def _merge(x, mod, ya, yb, hc, mw, *, tm, tiles_per_group):
    m, d = x.shape
    modspec, gspec = _mod_specs(mod, tm, lambda i: (i // tiles_per_group, i))
    full = lambda a: pl.BlockSpec(a.shape, lambda i: (0,) * a.ndim, pipeline_mode=pl.Buffered(1))
    rowspec = lambda n: pl.BlockSpec((tm, n), lambda i: (i, 0))
    ws = [mw["woc"], mw["wgt"], mw["wbr"], mw["wout"]]
    return pl.pallas_call(
        _merge_kernel, grid=(m // tm,),
        in_specs=[rowspec(d), modspec, gspec, rowspec(ya.shape[1]), rowspec(yb.shape[1]),
                  rowspec(hc.shape[1])] + [full(w) for w in ws],
        out_specs=rowspec(d),
        out_shape=jax.ShapeDtypeStruct((m, d), F32),
        compiler_params=_params(("arbitrary",)),
        name="merge",
    )(x, mod[0], mod[1], ya, yb, hc, *ws)


def _pick_tile(n, pref):
    t = min(n, pref)
    while n % t:
        t //= 2
    return t


def kernel(x_prompt, x_sample, cache_k, cache_v, cache_logf, state_C, state_n, state_m, page_table,
           c_prompt, c_sample, w_ada, b_ada, g_norm, w_ff_up, w_ff_down, w_in, g_va, w_s, b_s,
           g_qb, g_kb, b_fb, b_ic, b_fc, w_branch, w_out):
    B, S, D = x_prompt.shape
    Bd, T, _ = x_sample.shape
    depth = w_ada.shape[0]
    n_pool, psz, HB, dh = cache_k.shape[1:]
    HC, DV, DK = state_C.shape[2:]
    GA, chunk_a = w_s.shape[1], w_s.shape[2]
    DA = g_va.shape[1]
    DB = HB * dh
    DQK = HC * DK
    DC = HC * DV
    NBR, BW = w_branch.shape[1], w_branch.shape[2]
    n_sub = g_norm.shape[1]
    Ms = Bd * T

    splits = (DA, DA, DB, DB, DB, HB, DQK, DQK, DC, HC, HC, DC, NBR * D)
    offs = [0]
    for sz in splits:
        offs.append(offs[-1] + sz)
    (o_ua, o_va, o_qb, o_kb, o_vb, o_fb, o_qc, o_kc, o_vc, o_ic, o_fc, o_oc, o_gt, o_end) = offs

    w_in_t = jnp.swapaxes(w_in, 1, 2)
    nsg = LANES
    nsgt = 2 * ((HB + 2 * HC + 15) // 16) * 8
    n_small = HB + 2 * HC

    def cols(l, a, b):
        return jnp.swapaxes(w_in_t[l, a:b], 0, 1).astype(BF16)

    def rows(l, a, b):
        return w_in_t[l, a:b].astype(BF16)

    w_up16 = w_ff_up.astype(BF16)
    w_down16 = w_ff_down.astype(BF16)
    w_br16 = w_branch.astype(BF16)
    w_out16 = w_out.astype(BF16)

    sel_np = np.zeros((nsg, (HB // 2) * LANES), np.float32)
    for hd in range(HB):
        base = (hd // 2) * LANES + (hd % 2) * BIAS_LANES_PER_HEAD + BIAS_LANES_PER_HEAD // 2
        sel_np[hd, base:base + 3] = 1.0
    sel = jnp.asarray(sel_np, BF16)

    layer_w = []
    for l in range(depth):
        small_rows = jnp.concatenate([w_in_t[l, o_fb:o_qc], w_in_t[l, o_ic:o_oc]], axis=0)
        wst = jnp.zeros((nsgt, D), F32).at[:n_small].set(small_rows).astype(BF16)
        ws = jnp.zeros((D, nsg), F32).at[:, :n_small].set(small_rows.T).astype(BF16)
        bias_small = jnp.concatenate([b_fb[l], b_ic[l], b_fc[l]])
        bsg = jnp.zeros((1, nsg), F32).at[0, :n_small].set(bias_small)
        bsgt = jnp.zeros((nsgt, 1), F32).at[:n_small, 0].set(bias_small)
        common = dict(
            wa=cols(l, o_ua, o_qb), wk=cols(l, o_kb, o_vb),
            wc=jnp.concatenate([cols(l, o_qc, o_kc), cols(l, o_vc, o_ic)], axis=1), dvc=DC,
            ws=ws, gva=g_va[l].reshape(1, DA), gk_row=jnp.tile(g_kb[l], HB).reshape(1, DB), bsg=bsg)
        wblk = jnp.einsum("ab,gts->gatbs", jnp.eye(Bd, dtype=F32), w_s[l][:, :T, :T]).reshape(GA, Ms, Ms)
        pw_p = dict(common, wqt=rows(l, o_qb, o_kb), wkt=rows(l, o_kb, o_vb), wvt=rows(l, o_vb, o_fb),
                    wkct=rows(l, o_kc, o_vc), wst=wst, gq_col=g_qb[l].reshape(dh, 1), gk_col=g_kb[l].reshape(dh, 1),
                    bsgt=bsgt, sel=sel, wmix=w_s[l], bmix=b_s[l].T)
        pw_s = dict(common, wq=cols(l, o_qb, o_kb), wv=cols(l, o_vb, o_fb), wkc=cols(l, o_kc, o_vc),
                    gq_row=jnp.tile(g_qb[l], HB).reshape(1, DB), wmix=wblk,
                    bmix=jnp.tile(b_s[l][:, :T], (1, Bd)).T)
        mw = dict(woc=cols(l, o_oc, o_gt), wgt=cols(l, o_gt, o_end), wbr=w_br16[l], wout=w_out16[l])
        layer_w.append((pw_p, pw_s, mw))

    r_all = B + Bd
    r_pad = -(-r_all // 8) * 8
    c_all = jnp.zeros((r_pad, D), F32).at[:B].set(c_prompt).at[B:r_all].set(c_sample)
    mods = _ada_mod(c_all, w_ada, b_ada).reshape(depth, r_pad, n_sub, 3, D)
    mods_p = jnp.transpose(mods[:, :B], (0, 2, 1, 3, 4)).reshape(depth, n_sub, B, 3, 1, D)
    mods_s = jnp.transpose(jnp.repeat(mods[:, B:r_all], T, axis=1), (0, 2, 3, 1, 4))
    gains = g_norm.reshape(depth, n_sub, 1, D)

    cache_kt = jnp.transpose(cache_k, (0, 1, 3, 4, 2)).reshape(depth, n_pool, DB, psz)
    cache_vt = jnp.transpose(cache_v, (0, 1, 3, 4, 2)).reshape(depth, n_pool, DB, psz)
    cache_lft = jnp.transpose(cache_logf, (0, 1, 3, 2))

    tm_p = _pick_tile(S, 512)
    tm_proj = _pick_tile(S, 512)
    tk = _pick_tile(S, 256)
    tq = 2 * tk
    pages = _pick_tile(page_table.shape[1], 32)

    xp = x_prompt.reshape(B * S, D)
    xs = x_sample.reshape(Ms, D)
    outs_p, outs_s, chunk_v = [], [], []
    pr = None
    for l in range(depth):
        pw_p, pw_s, mw = layer_w[l]

        mod_p = lambda sub: (mods_p, gains, l, sub)
        mod_s = lambda sub: (mods_s, gains, l, sub)

        xp = _ffn(xp, mod_p(0), w_up16, w_down16, l, 0, tm_p, S // tm_p)
        pr = _proj(xp, mod_p(1), pw_p, prompt=True, batch=B, tm=tm_proj, tk=tk, chunk=chunk_a, groups=GA,
                   dh=dh, n_lf=HB, n_ic=HC, stacked=(l, depth, pr))
        yb =_fox_prompt(pr["qt16"], pr["ftc"], pr["kaug"], pr["vt16"], tq=tq, dh=dh)
        hc, st, mo = _mlstm(pr["qc"].reshape(B, S, DQK), pr["kct"], pr["vc"].reshape(B, S, DC),
                            pr["sg"].reshape(B, S, nsg),
                            jnp.zeros((B, DQK, DV), F32), jnp.zeros((B, DQK, 1), F32), jnp.zeros((B, HC, 1), F32),
                            n_heads=HC, dk=DK, dv=DV, valid=MLSTM_CHUNK, ic_col=HB, lf_col=HB + HC)
        xp = _merge(xp, mod_p(1), pr["ya"], yb, hc.reshape(B * S, DC), mw, tm=tm_proj,
                    tiles_per_group=S // tm_proj)
        xp = _ffn(xp, mod_p(2), w_up16, w_down16, l, 1, tm_p, S // tm_p)
        outs_p.append((
            jnp.swapaxes(st[:, :, :DV].reshape(B, HC, DK, DV), -1, -2),
            st[:, :, DV].reshape(B, HC, DK),
            mo[:, :, 0]))

        xs = _ffn(xs, mod_s(0), w_up16, w_down16, l, 0, Ms, 1)
        ps = _proj(xs, mod_s(1), pw_s, prompt=False, batch=1, tm=Ms, tk=Ms, chunk=Ms, groups=GA, dh=dh,
                   n_lf=HB, n_ic=HC)
        k32_s, v32_s, sg_s = ps["k32"], ps["v32"], ps["sg"]
        pad_t = lambda a: jnp.pad(a.reshape(Bd, T, -1), ((0, 0), (0, psz - T), (0, 0)))
        lft_new = jnp.swapaxes(pad_t(sg_s[:, :HB]), 1, 2)
        yb_s = _fox_sample(page_table, ps["q32"].reshape(Bd, T, DB), pad_t(k32_s).astype(BF16),
                           pad_t(v32_s).astype(BF16), lft_new, cache_kt, cache_vt, cache_lft, l,
                           n_heads=HB, dh=dh, pages=pages)
        padc = lambda a: jnp.pad(a.reshape(Bd, T, -1), ((0, 0), (0, MLSTM_CHUNK - T), (0, 0)))
        hc_s, st_s, mo_s = _mlstm(
            padc(ps["qc"]), jnp.swapaxes(padc(ps["kc"]), 1, 2), padc(ps["vc"]), padc(sg_s),
            jnp.swapaxes(state_C[l], -1, -2).reshape(Bd, DQK, DV), state_n[l].reshape(Bd, DQK, 1),
            state_m[l].reshape(Bd, HC, 1), n_heads=HC, dk=DK, dv=DV, valid=T, ic_col=HB, lf_col=HB + HC)
        xs = _merge(xs, mod_s(1), ps["ya"], yb_s.reshape(Ms, DB).astype(BF16), hc_s[:, :T].reshape(Ms, DC),
                    mw, tm=Ms, tiles_per_group=1)
        xs = _ffn(xs, mod_s(2), w_up16, w_down16, l, 1, Ms, 1)
        outs_s.append((
            k32_s.reshape(Bd, T, HB, dh), v32_s.reshape(Bd, T, HB, dh), sg_s[:, :HB].reshape(Bd, T, HB),
            jnp.swapaxes(st_s[:, :, :DV].reshape(Bd, HC, DK, DV), -1, -2),
            st_s[:, :, DV].reshape(Bd, HC, DK),
            mo_s[:, :, 0]))
        chunk_v.append(ps["va"].reshape(Bd, T, DA))

    stk = lambda states, i: jnp.stack([s[i] for s in states])
    return (xp.reshape(B, S, D), xs.reshape(Bd, T, D),
            jnp.transpose(pr["kt32"].reshape(depth, B, HB, dh, S), (0, 1, 4, 2, 3)),
            jnp.transpose(pr["vt32"].reshape(depth, B, HB, dh, S), (0, 1, 4, 2, 3)),
            jnp.transpose(pr["sgt"][:, :, :HB, :], (0, 1, 3, 2)),
            stk(outs_p, 0), stk(outs_p, 1), stk(outs_p, 2),
            stk(outs_s, 0), stk(outs_s, 1), stk(outs_s, 2), stk(outs_s, 3), stk(outs_s, 4), stk(outs_s, 5),
            jnp.stack(chunk_v))
```
